```python
import math
import jax, jax.numpy as jnp
from jax import lax
import numpy as np

D_MODEL = 2048
BATCH = 4
SEQ = 4096
DEPTH = 4
DEC_BATCH = 1
DEC_SEQ = 8192
PAST_LEN = 128

HEAD_DIM = 64
A_HEADS = 8
A_KV = 2
A_HALF_WIN = 128
B_HEADS = 8
B_PATTERNS = ((128, 1), (512, 4), (2048, 16))
C_HEADS = 4
C_VDIM = 2 * HEAD_DIM
D_HEADS = 8
D_KV = 2
GRID_W = 64
ROPE_THETA = 10000.0
ROPE_PAIRS = HEAD_DIM // 4
QBLK = 128
NUM_BUCKETS = 32
RELPOS_MAX_DIST = 1024
N_BIAS_HEADS = A_HEADS + B_HEADS + C_HEADS
FF_DIM = -(-8 * D_MODEL // (3 * 256)) * 256
N_MOD = 6

SPLIT_SIZES = (
    A_HEADS * HEAD_DIM, A_KV * HEAD_DIM, A_KV * HEAD_DIM,
    B_HEADS * HEAD_DIM, B_HEADS * HEAD_DIM, B_HEADS * HEAD_DIM,
    C_HEADS * 2 * HEAD_DIM, C_HEADS * 2 * HEAD_DIM, C_HEADS * C_VDIM,
    D_HEADS * HEAD_DIM, D_KV * HEAD_DIM, D_KV * HEAD_DIM,
)
IN_WIDTH = sum(SPLIT_SIZES)
MIX_WIDTH = A_HEADS * HEAD_DIM + B_HEADS * HEAD_DIM + C_HEADS * C_VDIM + D_HEADS * HEAD_DIM
NEG_INF = -1e30

kernel_name = "hybrid_parallel_head_group_encoder"


def rms_norm(x, g, eps=1e-6):
    xf = x.astype(jnp.float32)
    y = xf * lax.rsqrt(jnp.mean(xf * xf, axis=-1, keepdims=True) + eps)
    return (y * g.astype(jnp.float32)).astype(x.dtype)


def relpos_bucket(rel):
    half = NUM_BUCKETS // 2
    max_exact = half // 2
    n = jnp.abs(rel)
    nf = jnp.maximum(n, 1).astype(jnp.float32)
    large = max_exact + (jnp.log(nf / max_exact) / math.log(RELPOS_MAX_DIST / max_exact)
                         * (half - max_exact)).astype(jnp.int32)
    large = jnp.minimum(large, half - 1)
    return jnp.where(rel > 0, half, 0) + jnp.where(n < max_exact, n, large)


def relpos_bias(table, rel):
    return jnp.moveaxis(table[relpos_bucket(rel)].astype(jnp.float32), -1, 0)


def banded_attention(q, k, v, half_win, stride, bias_tab, sink=None):
    N, L, KV, G, d = q.shape
    blk = half_win
    nb = -(-L // blk)
    pad = nb * blk - L
    qb = jnp.pad(q, ((0, 0), (0, pad), (0, 0), (0, 0), (0, 0))).reshape(N, nb, blk, KV, G, d)

    def windows(t):
        tp = jnp.pad(t, ((0, 0), (blk, blk + pad), (0, 0), (0, 0))).reshape(N, nb + 2, blk, KV, t.shape[-1])
        return jnp.concatenate([tp[:, :-2], tp[:, 1:-1], tp[:, 2:]], axis=2)

    kw, vw = windows(k), windows(v)
    rel = jnp.arange(3 * blk)[None, :] - blk - jnp.arange(blk)[:, None]
    kpos = (jnp.arange(nb) * blk)[:, None, None] + jnp.arange(blk)[None, :, None] + rel[None]
    valid = (jnp.abs(rel) <= half_win)[None] & (kpos >= 0) & (kpos < L)
    bias = relpos_bias(bias_tab, rel * stride).reshape(KV, G, blk, 3 * blk)
    s = jnp.einsum('nbqhgd,nbkhd->nbhgqk', qb, kw, preferred_element_type=jnp.float32) * (d ** -0.5)
    s = jnp.where(valid[None, :, None, None], s + bias, NEG_INF)
    m = jnp.max(s, axis=-1, keepdims=True)
    if sink is not None:
        sk = sink.astype(jnp.float32).reshape(1, 1, KV, G, 1, 1)
        m = jnp.maximum(m, sk)
    p = jnp.exp(s - m)
    denom = jnp.sum(p, axis=-1, keepdims=True)
    if sink is not None:
        denom = denom + jnp.exp(sk - m)
    o = jnp.einsum('nbhgqk,nbkhe->nbqhge', (p / denom).astype(v.dtype), vw)
    lse = jnp.moveaxis((m + jnp.log(denom))[..., 0], -1, 2)
    o = o.reshape(N, nb * blk, KV, G, -1)[:, :L]
    lse = lse.reshape(N, nb * blk, KV, G)[:, :L]
    return o, lse


def to_strided(t, r):
    B, T = t.shape[:2]
    t = t.reshape(B, T // r, r, *t.shape[2:])
    return jnp.swapaxes(t, 1, 2).reshape(B * r, T // r, *t.shape[3:])


def from_strided(t, B):
    N, L = t.shape[:2]
    r = N // B
    t = t.reshape(B, r, L, *t.shape[2:])
    return jnp.swapaxes(t, 1, 2).reshape(B, L * r, *t.shape[3:])


def dilated_mixture(q, k, v, bias_tab):
    B = q.shape[0]
    outs, lses = [], []
    for window, r in B_PATTERNS:
        o, lse = banded_attention(to_strided(q, r)[:, :, :, None], to_strided(k, r), to_strided(v, r),
                                  window // (2 * r), r, bias_tab)
        outs.append(from_strided(o[:, :, :, 0], B))
        lses.append(from_strided(lse[..., 0], B))
    w = jax.nn.softmax(jnp.stack(lses), axis=0)
    out = jnp.einsum('pbth,pbthd->bthd', w, jnp.stack(outs).astype(jnp.float32))
    return out.astype(q.dtype)


def diff_attention(q, k, v, lam, bias_tab, subln_g, lam_init):
    B, T, H, _, d = q.shape
    nb = T // QBLK
    qb = jnp.moveaxis(q.reshape(B, nb, QBLK, H, 2, d), 1, 0)
    kpos = jnp.arange(T)

    def block(args):
        qblk, i = args
        rel = kpos[None, :] - (i * QBLK + jnp.arange(QBLK))[:, None]
        bias = relpos_bias(bias_tab, rel)
        s = jnp.einsum('bqhmd,bkhmd->bhmqk', qblk, k, preferred_element_type=jnp.float32) * (d ** -0.5)
        p = jax.nn.softmax(s + bias[None, :, None], axis=-1)
        a = p[:, :, 0] - lam * p[:, :, 1]
        return jnp.einsum('bhqk,bkhe->bqhe', a.astype(v.dtype), v)

    o = lax.map(block, (qb, jnp.arange(nb)))
    o = jnp.moveaxis(o, 0, 1).reshape(B, T, H, -1)
    return rms_norm(o, subln_g) * (1.0 - lam_init)


def dense_gqa(q, k, v):
    B, T, KV, G, d = q.shape
    nb = T // QBLK
    qb = jnp.moveaxis(q.reshape(B, nb, QBLK, KV, G, d), 1, 0)

    def block(qblk):
        s = jnp.einsum('bqhgd,bkhd->bhgqk', qblk, k, preferred_element_type=jnp.float32) * (d ** -0.5)
        p = jax.nn.softmax(s, axis=-1)
        return jnp.einsum('bhgqk,bkhd->bqhgd', p.astype(v.dtype), v)

    o = lax.map(block, qb)
    return jnp.moveaxis(o, 0, 1).reshape(B, T, KV * G * d)


def axial_rope_tables(T):
    rows = T // GRID_W
    r_idx, c_idx = jnp.meshgrid(jnp.arange(rows), jnp.arange(GRID_W), indexing='ij')
    pos = jnp.stack([r_idx.reshape(-1), c_idx.reshape(-1)], axis=-1).astype(jnp.float32)
    inv_freq = ROPE_THETA ** (-jnp.arange(ROPE_PAIRS, dtype=jnp.float32) / ROPE_PAIRS)
    ang = pos[:, :, None] * inv_freq
    return jnp.cos(ang), jnp.sin(ang)


def apply_axial_rope(x, cos, sin):
    xf = x.astype(jnp.float32).reshape(*x.shape[:-1], 2, 2, ROPE_PAIRS)
    x1, x2 = xf[..., 0, :], xf[..., 1, :]
    c, s = cos[:, None], sin[:, None]
    out = jnp.stack([x1 * c - x2 * s, x2 * c + x1 * s], axis=-2)
    return out.reshape(x.shape).astype(x.dtype)


def encoder_trunk(x, c, w_mod, b_mod, norm_mix, norm_ffn, w_in, w_out, qk_gain, sink_a,
                  relpos_table, diff_lambda, diff_subln, w_gate_up, w_down):
    B, T, _ = x.shape
    cos, sin = axial_rope_tables(T)
    tab_a = relpos_table[:, :A_HEADS]
    tab_b = relpos_table[:, A_HEADS:A_HEADS + B_HEADS]
    tab_c = relpos_table[:, A_HEADS + B_HEADS:]
    split_at = [int(i) for i in np.cumsum(SPLIT_SIZES)[:-1]]
    for l in range(DEPTH):
        mod = (jax.nn.silu(c) @ w_mod[l] + b_mod[l]).reshape(B, N_MOD, D_MODEL)[:, :, None, :]
        shift_m, scale_m, gate_m, shift_f, scale_f, gate_f = [mod[:, j] for j in range(N_MOD)]
        h = rms_norm(x, norm_mix[l]) * (1 + scale_m) + shift_m
        (qa, ka, va, qb, kb, vb, qc, kc, vc, qd, kd, vd) = jnp.split(h @ w_in[l], split_at, axis=-1)

        qa = rms_norm(qa.reshape(B, T, A_HEADS, HEAD_DIM), qk_gain[l, 0, 0]).reshape(
            B, T, A_KV, A_HEADS // A_KV, HEAD_DIM)
        ka = rms_norm(ka.reshape(B, T, A_KV, HEAD_DIM), qk_gain[l, 0, 1])
        out_a, _ = banded_attention(qa, ka, va.reshape(B, T, A_KV, HEAD_DIM), A_HALF_WIN, 1, tab_a,
                                    sink_a[l].reshape(A_KV, A_HEADS // A_KV))
        out_a = out_a.reshape(B, T, A_HEADS * HEAD_DIM)

        qb = rms_norm(qb.reshape(B, T, B_HEADS, HEAD_DIM), qk_gain[l, 1, 0])
        kb = rms_norm(kb.reshape(B, T, B_HEADS, HEAD_DIM), qk_gain[l, 1, 1])
        out_b = dilated_mixture(qb, kb, vb.reshape(B, T, B_HEADS, HEAD_DIM), tab_b).reshape(
            B, T, B_HEADS * HEAD_DIM)

        lam_init = 0.8 - 0.6 * math.exp(-0.3 * l)
        lv = diff_lambda[l].astype(jnp.float32)
        lam = jnp.exp(jnp.sum(lv[0] * lv[1])) - jnp.exp(jnp.sum(lv[2] * lv[3])) + lam_init
        qc = rms_norm(qc.reshape(B, T, C_HEADS, 2, HEAD_DIM), qk_gain[l, 2, 0])
        kc = rms_norm(kc.reshape(B, T, C_HEADS, 2, HEAD_DIM), qk_gain[l, 2, 1])
        out_c = diff_attention(qc, kc, vc.reshape(B, T, C_HEADS, C_VDIM), lam, tab_c, diff_subln[l],
                               lam_init).reshape(B, T, C_HEADS * C_VDIM)

        qd = apply_axial_rope(rms_norm(qd.reshape(B, T, D_HEADS, HEAD_DIM), qk_gain[l, 3, 0]), cos, sin)
        kd = apply_axial_rope(rms_norm(kd.reshape(B, T, D_KV, HEAD_DIM), qk_gain[l, 3, 1]), cos, sin)
        out_d = dense_gqa(qd.reshape(B, T, D_KV, D_HEADS // D_KV, HEAD_DIM), kd,
                          vd.reshape(B, T, D_KV, HEAD_DIM))

        mix = jnp.concatenate([out_a, out_b, out_c, out_d], axis=-1) @ w_out[l]
        x = x + gate_m * mix

        h = rms_norm(x, norm_ffn[l]) * (1 + scale_f) + shift_f
        g, u = jnp.split(h @ w_gate_up[l], 2, axis=-1)
        x = x + gate_f * ((jax.nn.silu(g) * u) @ w_down[l])
    return x


def setup_inputs(seed: int = 0) -> dict:
    key = jax.random.key(seed)
    ks = jax.random.split(key, 18)

    def nrm(k, shape, s):
        return jax.random.normal(k, shape, jnp.float32) * s

    b_mod = nrm(ks[5], (DEPTH, N_MOD, D_MODEL), 0.02)
    b_mod = b_mod.at[:, 2].add(1.0).at[:, 5].add(1.0).reshape(DEPTH, N_MOD * D_MODEL)
    return {
        "x_prompt": nrm(ks[0], (BATCH, SEQ, D_MODEL), 1.0),
        "x_sample": nrm(ks[1], (DEC_BATCH, DEC_SEQ, D_MODEL), 1.0),
        "c_prompt": nrm(ks[2], (BATCH, D_MODEL), 1.0),
        "c_sample": nrm(ks[3], (DEC_BATCH, D_MODEL), 1.0),
        "w_mod": nrm(ks[4], (DEPTH, D_MODEL, N_MOD * D_MODEL), 0.2 * D_MODEL ** -0.5),
        "b_mod": b_mod,
        "norm_mix": 1.0 + nrm(ks[6], (DEPTH, D_MODEL), 0.02),
        "norm_ffn": 1.0 + nrm(ks[7], (DEPTH, D_MODEL), 0.02),
        "w_in": nrm(ks[8], (DEPTH, D_MODEL, IN_WIDTH), D_MODEL ** -0.5),
        "w_out": nrm(ks[9], (DEPTH, MIX_WIDTH, D_MODEL), MIX_WIDTH ** -0.5),
        "qk_gain": 1.0 + nrm(ks[10], (DEPTH, 4, 2, HEAD_DIM), 0.02),
        "sink_a": nrm(ks[11], (DEPTH, A_HEADS), 0.5),
        "relpos_table": nrm(ks[12], (NUM_BUCKETS, N_BIAS_HEADS), 0.5),
        "diff_lambda": nrm(ks[13], (DEPTH, 4, HEAD_DIM), 0.1),
        "diff_subln": 1.0 + nrm(ks[14], (DEPTH, C_VDIM), 0.02),
        "w_gate_up": nrm(ks[15], (DEPTH, D_MODEL, 2 * FF_DIM), D_MODEL ** -0.5),
        "w_down": nrm(ks[16], (DEPTH, FF_DIM, D_MODEL), FF_DIM ** -0.5),
    }


def reference(x_prompt, x_sample, c_prompt, c_sample, w_mod, b_mod, norm_mix, norm_ffn, w_in, w_out,
              qk_gain, sink_a, relpos_table, diff_lambda, diff_subln, w_gate_up, w_down):
    weights = (w_mod, b_mod, norm_mix, norm_ffn, w_in, w_out, qk_gain, sink_a, relpos_table,
               diff_lambda, diff_subln, w_gate_up, w_down)
    y_prompt = encoder_trunk(x_prompt, c_prompt, *weights)
    y_sample = encoder_trunk(x_sample, c_sample, *weights)
    return (y_prompt, y_sample)
```

```python
import functools
import math

import numpy as np
import jax
import jax.numpy as jnp
from jax import lax
from jax.experimental import pallas as pl
from jax.experimental.pallas import tpu as pltpu

D_MODEL = 2048
DEPTH = 4
HEAD_DIM = 64
A_HEADS = 8
A_HALF_WIN = 128
B_HEADS = 8
B_PATTERNS = ((128, 1), (512, 4), (2048, 16))
C_HEADS = 4
C_VDIM = 128
GRID_W = 64
ROPE_THETA = 10000.0
ROPE_PAIRS = HEAD_DIM // 4
NUM_BUCKETS = 32
RELPOS_MAX_DIST = 1024
FF_DIM = 5632
N_MOD = 6
NEG_INF = -1e30
EPS = 1e-6

LANES = 128
P_WIDTH = 5120
N_SUB = P_WIDTH // 256
SUB_NORM = (True,) * 12 + (False,) * 4 + (True, True, False, False)
SUB_ROPE = (False,) * 6 + (True, True) + (False,) * 8 + (False, True, False, False)
COL_QA, COL_QB, COL_QC, COL_QD = 0, 512, 1024, 1536
COL_KB, COL_KC, COL_VB, COL_VC = 2048, 2560, 3072, 3584
COL_KA, COL_KD, COL_VA, COL_VD = 4096, 4352, 4608, 4864
C_BIAS_REACH = 6
VMEM_LIMIT = 56 * 1024 * 1024

F32 = jnp.float32
BF16 = jnp.bfloat16


def _cparams(sem):
    return pltpu.CompilerParams(dimension_semantics=sem, vmem_limit_bytes=VMEM_LIMIT)


def _mod_kernel(c_ref, w_ref, b_ref, o_ref):
    c = c_ref[...]
    a = (c * jax.nn.sigmoid(c)).astype(BF16)
    o_ref[0] = jnp.dot(a, w_ref[0].astype(BF16), preferred_element_type=F32) + b_ref[0]


def _mod_all(c_all, w_mod, b_mod):
    tn = 1024
    n = N_MOD * D_MODEL
    return pl.pallas_call(
        _mod_kernel,
        out_shape=jax.ShapeDtypeStruct((DEPTH, 8, n), F32),
        grid=(DEPTH, n // tn),
        in_specs=[
            pl.BlockSpec((8, D_MODEL), lambda l, j: (0, 0)),
            pl.BlockSpec((1, D_MODEL, tn), lambda l, j: (l, 0, j)),
            pl.BlockSpec((1, 1, tn), lambda l, j: (l, 0, j)),
        ],
        out_specs=pl.BlockSpec((1, 8, tn), lambda l, j: (l, 0, j)),
        compiler_params=_cparams(("parallel", "parallel")),
        name="mod_proj",
    )(c_all, w_mod, b_mod.reshape(DEPTH, 1, n))


def _modulated_norm(x, g, shift, scale):
    ms = jnp.mean(x * x, axis=-1, keepdims=True)
    h = x * lax.rsqrt(ms + EPS) * g
    return h * (1.0 + scale) + shift


def _proj_in_kernel(x_ref, mod_ref, g_ref, w_ref, gain_ref, bd_ref, cos_ref, sin_ref, o_ref):
    mod = mod_ref[0]
    hb = _modulated_norm(x_ref[...], g_ref[...], mod[0:1], mod[1:2]).astype(BF16)
    lane = lax.broadcasted_iota(jnp.int32, (1, 256), 1)
    upper16 = (lane & 16) != 0
    bd = bd_ref[...]
    for c in range(P_WIDTH // 512):
        r = jnp.dot(hb, w_ref[:, c * 512:(c + 1) * 512], preferred_element_type=F32)
        for half in range(2):
            sub = 2 * c + half
            y = r[:, half * 256:(half + 1) * 256]
            if SUB_NORM[sub]:
                sq = y * y
                hi = sq.astype(BF16)
                lo = (sq - hi.astype(F32)).astype(BF16)
                msq = (jnp.dot(hi, bd, preferred_element_type=F32)
                       + jnp.dot(lo, bd, preferred_element_type=F32))
                y = y * lax.rsqrt(msq + EPS) * gain_ref[:, sub * 256:(sub + 1) * 256]
            if SUB_ROPE[sub]:
                cs = jnp.concatenate([cos_ref[...]] * 2, axis=1)
                sn = jnp.concatenate([sin_ref[...]] * 2, axis=1)
                partner = jnp.where(upper16, pltpu.roll(y, 16, 1), pltpu.roll(y, 256 - 16, 1))
                y = y * cs + partner * sn
            o_ref[:, sub * 256:(sub + 1) * 256] = y.astype(BF16)


def _proj_in(x2d, mod, g, w, gain, bd, cos_t, sin_t, T):
    M = x2d.shape[0]
    tm = 512
    nt = T // tm
    return pl.pallas_call(
        _proj_in_kernel,
        out_shape=jax.ShapeDtypeStruct((M, P_WIDTH), BF16),
        grid=(M // tm,),
        in_specs=[
            pl.BlockSpec((tm, D_MODEL), lambda i: (i, 0)),
            pl.BlockSpec((1, N_MOD, D_MODEL), lambda i: (i // nt, 0, 0)),
            pl.BlockSpec((1, D_MODEL), lambda i: (0, 0)),
            pl.BlockSpec((D_MODEL, P_WIDTH), lambda i: (0, 0), pipeline_mode=pl.Buffered(1)),
            pl.BlockSpec((1, P_WIDTH), lambda i: (0, 0)),
            pl.BlockSpec((256, 256), lambda i: (0, 0)),
            pl.BlockSpec((tm, LANES), lambda i: (i % nt, 0)),
            pl.BlockSpec((tm, LANES), lambda i: (i % nt, 0)),
        ],
        out_specs=pl.BlockSpec((tm, P_WIDTH), lambda i: (i, 0)),
        compiler_params=_cparams(("parallel",)),
        name="proj_in",
    )(x2d, mod, g, w, gain, bd, cos_t, sin_t)


def _banded_kernel(*refs, tq, hw, L, gqa, has_sink, want_lse):
    q_ref, kp_ref, kc_ref, kn_ref, vp_ref, vc_ref, vn_ref, bias_ref = refs[:8]
    rest = refs[8:]
    if has_sink:
        sink_ref, rest = rest[0], rest[1:]
    o_ref = rest[0]
    lse_ref = rest[1] if want_lse else None

    i = pl.program_id(2)
    W = tq + 2 * hw
    q = q_ref[...]
    kwin = jnp.concatenate([kp_ref[...], kc_ref[...], kn_ref[...]], axis=0)
    vwin = jnp.concatenate([vp_ref[...], vc_ref[...], vn_ref[...]], axis=0)
    kpos = i * tq - hw + lax.broadcasted_iota(jnp.int32, (1, W), 1)
    valid = jnp.where(kpos >= 0, kpos, L) < L
    lower = lax.broadcasted_iota(jnp.int32, (1, LANES), 1) < HEAD_DIM
    zero = jnp.zeros((), BF16)
    for jp in range(4):
        qp = q[:, jp * LANES:(jp + 1) * LANES]
        kb = jp // 2 if gqa else jp
        kk = kwin[:, kb * LANES:(kb + 1) * LANES]
        vv = vwin[:, kb * LANES:(kb + 1) * LANES]
        o_pair = None
        lse_pair = None
        for e in range(2):
            h = 2 * jp + e
            sel = lower if e == 0 else jnp.logical_not(lower)
            qe = jnp.where(sel, qp, zero)
            s = lax.dot_general(qe, kk, (((1,), (1,)), ((), ())), preferred_element_type=F32)
            s = jnp.where(valid, s + bias_ref[h], NEG_INF)
            m = jnp.max(s, axis=1, keepdims=True)
            if has_sink:
                sk = sink_ref[h]
                m = jnp.maximum(m, sk)
            p = jnp.exp(s - m)
            d = jnp.sum(p, axis=1, keepdims=True)
            if has_sink:
                d = d + jnp.exp(sk - m)
            ve = jnp.where(sel, vv, zero)
            oe = jnp.dot(p.astype(BF16), ve, preferred_element_type=F32) / d
            o_pair = oe if o_pair is None else o_pair + oe
            if want_lse:
                lse_e = jnp.broadcast_to(m + jnp.log(d), (tq, LANES))
                lse_pair = lse_e if lse_pair is None else jnp.where(lower, lse_pair, lse_e)
        o_ref[:, jp * LANES:(jp + 1) * LANES] = o_pair.astype(o_ref.dtype)
        if want_lse:
            lse_ref[:, jp * LANES:(jp + 1) * LANES] = lse_pair


def _banded_attention(P, B, T, r, hw, col_q, col_k, col_v, gqa, bias, sink, out_dtype, want_lse):
    L = T // r
    tq = min(256, L)
    kw = 256 if gqa else 512
    Pv = P.reshape(B * L, r * P_WIDTH)
    nq = L // tq
    sub = tq // hw
    nh = L // hw

    def q_map(b, rho, i):
        return (b * nq + i, rho * (P_WIDTH // 512) + col_q // 512)

    def cur_map(col):
        return lambda b, rho, i: (b * nq + i, (rho * P_WIDTH + col) // kw)

    def prev_map(col):
        return lambda b, rho, i: (b * nh + jnp.maximum(i * sub - 1, 0), (rho * P_WIDTH + col) // kw)

    def next_map(col):
        return lambda b, rho, i: (b * nh + jnp.minimum((i + 1) * sub, nh - 1), (rho * P_WIDTH + col) // kw)

    in_specs = [
        pl.BlockSpec((tq, 512), q_map),
        pl.BlockSpec((hw, kw), prev_map(col_k)),
        pl.BlockSpec((tq, kw), cur_map(col_k)),
        pl.BlockSpec((hw, kw), next_map(col_k)),
        pl.BlockSpec((hw, kw), prev_map(col_v)),
        pl.BlockSpec((tq, kw), cur_map(col_v)),
        pl.BlockSpec((hw, kw), next_map(col_v)),
        pl.BlockSpec(bias.shape, lambda b, rho, i: (0, 0, 0)),
    ]
    args = [Pv, Pv, Pv, Pv, Pv, Pv, Pv, bias]
    if sink is not None:
        in_specs.append(pl.BlockSpec(memory_space=pltpu.SMEM))
        args.append(sink)
    o_spec = pl.BlockSpec((tq, 512), lambda b, rho, i: (b * nq + i, rho))
    out_shape = [jax.ShapeDtypeStruct((B * L, r * 512), out_dtype)]
    out_specs = [o_spec]
    if want_lse:
        out_shape.append(jax.ShapeDtypeStruct((B * L, r * 512), F32))
        out_specs.append(o_spec)
    outs = pl.pallas_call(
        functools.partial(_banded_kernel, tq=tq, hw=hw, L=L, gqa=gqa, has_sink=sink is not None,
                          want_lse=want_lse),
        out_shape=out_shape,
        grid=(B, r, nq),
        in_specs=in_specs,
        out_specs=out_specs,
        compiler_params=_cparams(("parallel", "parallel", "parallel")),
        name=f"banded_r{r}_hw{hw}",
    )(*args)
    return [o.reshape(B * T, 512) for o in outs]


def _combine_kernel(o1_ref, o2_ref, o3_ref, l1_ref, l2_ref, l3_ref, out_ref):
    l1, l2, l3 = l1_ref[...], l2_ref[...], l3_ref[...]
    m = jnp.maximum(jnp.maximum(l1, l2), l3)
    e1, e2, e3 = jnp.exp(l1 - m), jnp.exp(l2 - m), jnp.exp(l3 - m)
    acc = e1 * o1_ref[...] + e2 * o2_ref[...] + e3 * o3_ref[...]
    out_ref[...] = (acc / (e1 + e2 + e3)).astype(out_ref.dtype)


def _combine(os_, ls_):
    M = os_[0].shape[0]
    tm = 512
    spec = pl.BlockSpec((tm, 512), lambda i: (i, 0))
    return pl.pallas_call(
        _combine_kernel,
        out_shape=jax.ShapeDtypeStruct((M, 512), BF16),
        grid=(M // tm,),
        in_specs=[spec] * 6,
        out_specs=spec,
        compiler_params=_cparams(("parallel",)),
        name="dilated_combine",
    )(*os_, *ls_)


def _online_softmax_step(s, v, m_scr, l_scr, acc_scr, tk):
    m_prev = m_scr[...]
    m_next = jnp.maximum(m_prev, jnp.max(s, axis=1, keepdims=True))
    p = jnp.exp(s - jnp.concatenate([m_next] * (tk // LANES), axis=1))
    alpha = jnp.exp(m_prev - m_next)
    l_scr[...] = alpha * l_scr[...] + jnp.sum(p, axis=1, keepdims=True)
    acc_scr[...] = alpha * acc_scr[...] + jnp.dot(p.astype(BF16), v, preferred_element_type=F32)
    m_scr[...] = m_next


def _split_pair(qp, lower):
    zero = jnp.zeros((), BF16)
    return jnp.where(lower, qp, zero), jnp.where(lower, zero, qp)


def _dense_gqa_kernel(q_ref, k_ref, v_ref, o_ref, m_scr, l_scr, acc_scr, *, tq, tk, T):
    q = q_ref[...]
    lower = lax.broadcasted_iota(jnp.int32, (1, LANES), 1) < HEAD_DIM
    for g in range(2):
        parts = []
        for jp in (2 * g, 2 * g + 1):
            parts.extend(_split_pair(q[:, jp * LANES:(jp + 1) * LANES], lower))
        lhs = jnp.concatenate(parts, axis=0)
        m_scr[...] = jnp.full(m_scr.shape, NEG_INF, F32)
        l_scr[...] = jnp.zeros(l_scr.shape, F32)
        acc_scr[...] = jnp.zeros(acc_scr.shape, F32)

        def body(c, carry):
            off = pl.multiple_of(c * tk, tk)
            kc = k_ref[pl.ds(off, tk), g * LANES:(g + 1) * LANES]
            vc = v_ref[pl.ds(off, tk), g * LANES:(g + 1) * LANES]
            s = lax.dot_general(lhs, kc, (((1,), (1,)), ((), ())), preferred_element_type=F32)
            _online_softmax_step(s, vc, m_scr, l_scr, acc_scr, tk)
            return carry

        lax.fori_loop(0, T // tk, body, 0)
        o = acc_scr[...] / l_scr[...]
        for jj in range(2):
            jp = 2 * g + jj
            even = o[(2 * jj) * tq:(2 * jj + 1) * tq]
            odd = o[(2 * jj + 1) * tq:(2 * jj + 2) * tq]
            o_ref[:, jp * LANES:(jp + 1) * LANES] = jnp.where(lower, even, odd).astype(o_ref.dtype)


def _dense_gqa(P, B, T):
    tq = 128
    tk = min(512, T)
    nq = T // tq
    rows = 4 * tq
    return pl.pallas_call(
        functools.partial(_dense_gqa_kernel, tq=tq, tk=tk, T=T),
        out_shape=jax.ShapeDtypeStruct((B * T, 512), BF16),
        grid=(B, nq),
        in_specs=[
            pl.BlockSpec((tq, 512), lambda b, i: (b * nq + i, COL_QD // 512)),
            pl.BlockSpec((T, 256), lambda b, i: (b, COL_KD // 256)),
            pl.BlockSpec((T, 256), lambda b, i: (b, COL_VD // 256)),
        ],
        out_specs=pl.BlockSpec((tq, 512), lambda b, i: (b * nq + i, 0)),
        scratch_shapes=[pltpu.VMEM((rows, LANES), F32)] * 3,
        compiler_params=_cparams(("parallel", "parallel")),
        name="dense_gqa",
    )(P, P, P)


def _diff_kernel(q_ref, k_ref, v_ref, bias_ref, lamv_ref, subln_ref, o_ref, m_scr, l_scr, acc_scr, *,
                 tq, tk, T, lam_init):
    i = pl.program_id(2)
    lower = lax.broadcasted_iota(jnp.int32, (1, LANES), 1) < HEAD_DIM
    lhs = jnp.concatenate(_split_pair(q_ref[...], lower), axis=0)
    m_scr[...] = jnp.full(m_scr.shape, NEG_INF, F32)
    l_scr[...] = jnp.zeros(l_scr.shape, F32)
    acc_scr[...] = jnp.zeros(acc_scr.shape, F32)
    nt = tk // tq

    def body(c, carry):
        off = pl.multiple_of(c * tk, tk)
        kc = k_ref[pl.ds(off, tk), :]
        vc = v_ref[pl.ds(off, tk), :]
        s = lax.dot_general(lhs, kc, (((1,), (1,)), ((), ())), preferred_element_type=F32)
        tiles = []
        for t in range(nt):
            delta = jnp.clip(c * nt + t - i, -C_BIAS_REACH, C_BIAS_REACH) + C_BIAS_REACH
            tiles.append(bias_ref[delta])
        b = jnp.concatenate(tiles, axis=1)
        s = s + jnp.concatenate([b, b], axis=0)
        _online_softmax_step(s, vc, m_scr, l_scr, acc_scr, tk)
        return carry

    lax.fori_loop(0, T // tk, body, 0)
    o = acc_scr[...] / l_scr[...]
    lv = lamv_ref[...]
    lam = (jnp.exp(jnp.sum(lv[0:1] * lv[1:2], axis=1, keepdims=True))
           - jnp.exp(jnp.sum(lv[2:3] * lv[3:4], axis=1, keepdims=True)) + lam_init)
    o = o[:tq] - lam * o[tq:]
    y = o * lax.rsqrt(jnp.mean(o * o, axis=-1, keepdims=True) + EPS) * subln_ref[...]
    o_ref[...] = (y * (1.0 - lam_init)).astype(o_ref.dtype)


def _diff_attention(P, B, T, bias_c, lamv, subln, lam_init):
    tq = 128
    tk = min(512, T)
    nq = T // tq
    return pl.pallas_call(
        functools.partial(_diff_kernel, tq=tq, tk=tk, T=T, lam_init=lam_init),
        out_shape=jax.ShapeDtypeStruct((B * T, 512), BF16),
        grid=(B, C_HEADS, nq),
        in_specs=[
            pl.BlockSpec((tq, LANES), lambda b, h, i: (b * nq + i, COL_QC // LANES + h)),
            pl.BlockSpec((T, LANES), lambda b, h, i: (b, COL_KC // LANES + h)),
            pl.BlockSpec((T, LANES), lambda b, h, i: (b, COL_VC // LANES + h)),
            pl.BlockSpec((None, 2 * C_BIAS_REACH + 1, tq, tq), lambda b, h, i: (h, 0, 0, 0)),
            pl.BlockSpec((4, HEAD_DIM), lambda b, h, i: (0, 0)),
            pl.BlockSpec((1, C_VDIM), lambda b, h, i: (0, 0)),
        ],
        out_specs=pl.BlockSpec((tq, LANES), lambda b, h, i: (b * nq + i, h)),
        scratch_shapes=[pltpu.VMEM((2 * tq, LANES), F32)] * 3,
        compiler_params=_cparams(("parallel", "parallel", "parallel")),
        name="diff_attn",
    )(P, P, P, bias_c, lamv, subln)


def _gated_out_kernel(*refs, n_a, gate_row):
    a_refs = refs[:n_a]
    w_ref, x_ref, mod_ref, o_ref = refs[n_a:]
    a = a_refs[0][...] if n_a == 1 else jnp.concatenate([r[...] for r in a_refs], axis=1)
    y = jnp.dot(a, w_ref[...], preferred_element_type=F32)
    o_ref[...] = x_ref[...] + mod_ref[0, gate_row:gate_row + 1, :] * y


def _gated_out(a_list, w, x2d, mod, gate_row, T):
    M = x2d.shape[0]
    tm, tn = 1024, 512
    nt = T // tm
    a_specs = [pl.BlockSpec((tm, a.shape[1]), lambda i, j: (i, 0)) for a in a_list]
    return pl.pallas_call(
        functools.partial(_gated_out_kernel, n_a=len(a_list), gate_row=gate_row),
        out_shape=jax.ShapeDtypeStruct((M, D_MODEL), F32),
        grid=(M // tm, D_MODEL // tn),
        in_specs=a_specs + [
            pl.BlockSpec((w.shape[0], tn), lambda i, j: (0, j)),
            pl.BlockSpec((tm, tn), lambda i, j: (i, j)),
            pl.BlockSpec((1, N_MOD, tn), lambda i, j: (i // nt, 0, j)),
        ],
        out_specs=pl.BlockSpec((tm, tn), lambda i, j: (i, j)),
        compiler_params=_cparams(("parallel", "arbitrary")),
        name=f"gated_out_{gate_row}",
    )(*a_list, w, x2d, mod)


def _ffn_up_kernel(x_ref, mod_ref, g_ref, wg_ref, wu_ref, o_ref, h_scr):
    @pl.when(pl.program_id(1) == 0)
    def _():
        mod = mod_ref[0]
        h_scr[...] = _modulated_norm(x_ref[...], g_ref[...], mod[3:4], mod[4:5]).astype(BF16)

    hb = h_scr[...]
    gate = jnp.dot(hb, wg_ref[...], preferred_element_type=F32)
    up = jnp.dot(hb, wu_ref[...], preferred_element_type=F32)
    o_ref[...] = (gate * jax.nn.sigmoid(gate) * up).astype(o_ref.dtype)


def _ffn_up(x2d, mod, g, w_gu, T):
    M = x2d.shape[0]
    tm, tn = 1024, 512
    nt = T // tm
    nj = FF_DIM // tn
    return pl.pallas_call(
        _ffn_up_kernel,
        out_shape=jax.ShapeDtypeStruct((M, FF_DIM), BF16),
        grid=(M // tm, nj),
        in_specs=[
            pl.BlockSpec((tm, D_MODEL), lambda i, j: (i, 0)),
            pl.BlockSpec((1, N_MOD, D_MODEL), lambda i, j: (i // nt, 0, 0)),
            pl.BlockSpec((1, D_MODEL), lambda i, j: (0, 0)),
            pl.BlockSpec((D_MODEL, tn), lambda i, j: (0, j)),
            pl.BlockSpec((D_MODEL, tn), lambda i, j: (0, j + nj)),
        ],
        out_specs=pl.BlockSpec((tm, tn), lambda i, j: (i, j)),
        scratch_shapes=[pltpu.VMEM((tm, D_MODEL), BF16)],
        compiler_params=_cparams(("parallel", "arbitrary")),
        name="ffn_up",
    )(x2d, mod, g, w_gu, w_gu)


def _relpos_bucket(rel):
    half = NUM_BUCKETS // 2
    max_exact = half // 2
    n = jnp.abs(rel)
    nf = jnp.maximum(n, 1).astype(F32)
    large = max_exact + (jnp.log(nf / max_exact) / math.log(RELPOS_MAX_DIST / max_exact)
                         * (half - max_exact)).astype(jnp.int32)
    large = jnp.minimum(large, half - 1)
    return jnp.where(rel > 0, half, 0) + jnp.where(n < max_exact, n, large)


def _band_bias(table, tq, hw, stride):
    rel = jnp.arange(tq + 2 * hw)[None, :] - hw - jnp.arange(tq)[:, None]
    bias = jnp.moveaxis(table[_relpos_bucket(rel * stride)].astype(F32), -1, 0)
    return jnp.where((jnp.abs(rel) <= hw)[None], bias, NEG_INF)


def _dense_bias(table, blk):
    delta = jnp.arange(-C_BIAS_REACH, C_BIAS_REACH + 1)
    rel = delta[:, None, None] * blk + jnp.arange(blk)[None, None, :] - jnp.arange(blk)[None, :, None]
    return jnp.moveaxis(table[_relpos_bucket(rel)].astype(F32), -1, 0)


def _rope_tables(T):
    t = jnp.arange(T)
    pos = jnp.stack([t // GRID_W, t % GRID_W], axis=-1).astype(F32)
    inv_freq = ROPE_THETA ** (-jnp.arange(ROPE_PAIRS, dtype=F32) / ROPE_PAIRS)
    ang = pos[:, :, None] * inv_freq
    d = np.arange(LANES) % HEAD_DIM
    axis, half, pair = d // 32, (d % 32) // 16, d % 16
    cos_t = jnp.cos(ang)[:, axis, pair]
    sin_t = jnp.sin(ang)[:, axis, pair] * jnp.asarray(np.where(half == 0, -1.0, 1.0), F32)
    return cos_t, sin_t


def _dup_kv(w, off):
    h0, h1 = w[..., off:off + 64], w[..., off + 64:off + 128]
    return [h0, h0, h1, h1]


def _prep_w_in(w_in):
    s = lambda a, b: w_in[..., a:b]
    cols = [s(0, 512), s(768, 1280), s(2304, 2816), s(3840, 4352),
            s(1280, 1792), s(2816, 3328), s(1792, 2304), s(3328, 3840)]
    cols += _dup_kv(w_in, 512) + _dup_kv(w_in, 4352) + _dup_kv(w_in, 640) + _dup_kv(w_in, 4480)
    return jnp.concatenate(cols, axis=-1).astype(BF16)


def _prep_gain(qk_gain):
    qs = HEAD_DIM ** -0.5
    t = lambda g, n: jnp.tile(g, (1, n))
    ones = jnp.ones((DEPTH, 512), F32)
    rows = [t(qk_gain[:, 0, 0], 8) * qs, t(qk_gain[:, 1, 0], 8) * qs, t(qk_gain[:, 2, 0], 8) * qs,
            t(qk_gain[:, 3, 0], 8) * qs, t(qk_gain[:, 1, 1], 8), t(qk_gain[:, 2, 1], 8), ones, ones,
            t(qk_gain[:, 0, 1], 4), t(qk_gain[:, 3, 1], 4), ones]
    return jnp.concatenate(rows, axis=-1).reshape(DEPTH, 1, P_WIDTH)


def _trunk(x, mods, consts, weights):
    B, T, _ = x.shape
    M = B * T
    (norm_mix, norm_ffn, w_in_p, gain, w_out, sink_a, diff_lambda, diff_subln, w_gu, w_down) = weights
    bd, bias_a, bias_b, bias_c = consts
    cos_t, sin_t = _rope_tables(T)
    x2d = x.reshape(M, D_MODEL)
    for l in range(DEPTH):
        mod = mods[l]
        P = _proj_in(x2d, mod, norm_mix[l][None], w_in_p[l], gain[l], bd, cos_t, sin_t, T)
        (oa,) = _banded_attention(P, B, T, 1, A_HALF_WIN, COL_QA, COL_KA, COL_VA, True, bias_a,
                                  sink_a[l], BF16, False)
        os_, ls_ = [], []
        for (window, r), bias in zip(B_PATTERNS, bias_b):
            o, lse = _banded_attention(P, B, T, r, window // (2 * r), COL_QB, COL_KB, COL_VB, False,
                                       bias, None, F32, True)
            os_.append(o)
            ls_.append(lse)
        ob = _combine(os_, ls_)
        lam_init = 0.8 - 0.6 * math.exp(-0.3 * l)
        oc = _diff_attention(P, B, T, bias_c, diff_lambda[l], diff_subln[l][None], lam_init)
        od = _dense_gqa(P, B, T)
        x2d = _gated_out([oa, ob, oc, od], w_out[l], x2d, mod, 2, T)
        hf = _ffn_up(x2d, mod, norm_ffn[l][None], w_gu[l], T)
        x2d = _gated_out([hf], w_down[l], x2d, mod, 5, T)
    return x2d.reshape(B, T, D_MODEL)


def kernel(x_prompt, x_sample, c_prompt, c_sample, w_mod, b_mod, norm_mix, norm_ffn, w_in, w_out, qk_gain,
           sink_a, relpos_table, diff_lambda, diff_subln, w_gate_up, w_down):
    nb = x_prompt.shape[0]
    ns = x_sample.shape[0]
    c_all = jnp.concatenate([c_prompt, c_sample, jnp.zeros((8 - nb - ns, D_MODEL), F32)], axis=0)
    mods = _mod_all(c_all, w_mod, b_mod).reshape(DEPTH, 8, N_MOD, D_MODEL)

    idx = np.arange(256)
    bd = jnp.asarray((idx[:, None] // HEAD_DIM == idx[None, :] // HEAD_DIM) / HEAD_DIM, BF16)
    tab_a = relpos_table[:, :A_HEADS]
    tab_b = relpos_table[:, A_HEADS:A_HEADS + B_HEADS]
    tab_c = relpos_table[:, A_HEADS + B_HEADS:]
    weights = (norm_mix, norm_ffn, _prep_w_in(w_in), _prep_gain(qk_gain), w_out.astype(BF16), sink_a,
               diff_lambda, diff_subln, w_gate_up.astype(BF16), w_down.astype(BF16))

    outs = []
    for x, lo, hi in ((x_prompt, 0, nb), (x_sample, nb, nb + ns)):
        T = x.shape[1]
        bias_a = _band_bias(tab_a, min(256, T), A_HALF_WIN, 1)
        bias_b = [_band_bias(tab_b, min(256, T // r), w // (2 * r), r) for w, r in B_PATTERNS]
        bias_c = _dense_bias(tab_c, 128)
        outs.append(_trunk(x, mods[:, lo:hi], (bd, bias_a, bias_b, bias_c), weights))
    return tuple(outs)
```

```python
import functools
import math

import numpy as np
import jax
import jax.numpy as jnp
from jax import lax
from jax.experimental import pallas as pl
from jax.experimental.pallas import tpu as pltpu

D_MODEL = 2048
DEPTH = 4
HEAD_DIM = 64
A_HEADS = 8
A_HALF_WIN = 128
B_HEADS = 8
B_PATTERNS = ((128, 1), (512, 4), (2048, 16))
C_HEADS = 4
C_VDIM = 128
GRID_W = 64
ROPE_THETA = 10000.0
ROPE_PAIRS = HEAD_DIM // 4
NUM_BUCKETS = 32
RELPOS_MAX_DIST = 1024
FF_DIM = 5632
N_MOD = 6
NEG_INF = -1e30
EPS = 1e-6
LOG2E = math.log2(math.e)

LANES = 128
P_WIDTH = 5120
N_SUB = P_WIDTH // 256
SUB_NORM = (True,) * 12 + (False,) * 4 + (True, True, False, False)
SUB_ROPE = (False,) * 6 + (True, True) + (False,) * 8 + (False, True, False, False)
SUB_ONES = (False,) * 19 + (True,)
B_CHUNKS = (1, 4, 6)
COL_QA, COL_QB, COL_QC, COL_QD = 0, 512, 1024, 1536
COL_KB, COL_KC, COL_VB, COL_VC = 2048, 2560, 3072, 3584
COL_KA, COL_KD, COL_VA, COL_VD = 4096, 4352, 4608, 4864
C_BIAS_REACH = 6
VMEM_LIMIT = 56 * 1024 * 1024

F32 = jnp.float32
BF16 = jnp.bfloat16


def _cparams(sem):
    return pltpu.CompilerParams(dimension_semantics=sem, vmem_limit_bytes=VMEM_LIMIT)


def _mod_kernel(c_ref, w_ref, b_ref, o_ref):
    c = c_ref[...]
    a = (c * jax.nn.sigmoid(c)).astype(BF16)
    o_ref[0] = jnp.dot(a, w_ref[0].astype(BF16), preferred_element_type=F32) + b_ref[0]


def _mod_all(c_all, w_mod, b_mod):
    tn = 1024
    n = N_MOD * D_MODEL
    return pl.pallas_call(
        _mod_kernel,
        out_shape=jax.ShapeDtypeStruct((DEPTH, 8, n), F32),
        grid=(DEPTH, n // tn),
        in_specs=[
            pl.BlockSpec((8, D_MODEL), lambda l, j: (0, 0)),
            pl.BlockSpec((1, D_MODEL, tn), lambda l, j: (l, 0, j)),
            pl.BlockSpec((1, 1, tn), lambda l, j: (l, 0, j)),
        ],
        out_specs=pl.BlockSpec((1, 8, tn), lambda l, j: (l, 0, j)),
        compiler_params=_cparams(("parallel", "parallel")),
        name="mod_proj",
    )(c_all, w_mod, b_mod.reshape(DEPTH, 1, n))


def _modulated_norm(x, g, shift, scale):
    ms = jnp.mean(x * x, axis=-1, keepdims=True)
    h = x * lax.rsqrt(ms + EPS) * g
    return h * (1.0 + scale) + shift


def _proj_in_kernel(x_ref, mod_ref, g_ref, w_ref, gain_ref, bd_ref, cos_ref, sin_ref, o_ref, *rest):
    strided_refs, ybuf = rest[:-1], rest[-1]
    tm = x_ref.shape[0]
    mod = mod_ref[0]
    hb = _modulated_norm(x_ref[...], g_ref[...], mod[0:1], mod[1:2]).astype(BF16)
    lane = lax.broadcasted_iota(jnp.int32, (1, 256), 1)
    upper16 = (lane & 16) != 0
    bd = bd_ref[...]
    for c in range(P_WIDTH // 512):
        r = jnp.dot(hb, w_ref[:, c * 512:(c + 1) * 512], preferred_element_type=F32)
        for half in range(2):
            sub = 2 * c + half
            y = r[:, half * 256:(half + 1) * 256]
            if SUB_NORM[sub]:
                sq = y * y
                hi = sq.astype(BF16)
                lo = (sq - hi.astype(F32)).astype(BF16)
                msq = (jnp.dot(hi, bd, preferred_element_type=F32)
                       + jnp.dot(lo, bd, preferred_element_type=F32))
                y = y * lax.rsqrt(msq + EPS) * gain_ref[:, sub * 256:(sub + 1) * 256]
            if SUB_ROPE[sub]:
                cs = jnp.concatenate([cos_ref[...]] * 2, axis=1)
                sn = jnp.concatenate([sin_ref[...]] * 2, axis=1)
                partner = jnp.where(upper16, pltpu.roll(y, 16, 1), pltpu.roll(y, 256 - 16, 1))
                y = y * cs + partner * sn
            if SUB_ONES[sub]:
                y = jnp.where((lane & HEAD_DIM) != 0, 1.0, y)
            o_ref[:, sub * 256:(sub + 1) * 256] = y.astype(BF16)
            if c in B_CHUNKS:
                for j in range(2):
                    ybuf[2 * half + j] = y[:, j * LANES:(j + 1) * LANES]
        if c in B_CHUNKS:
            col = B_CHUNKS.index(c) * 512
            for (_, r), s_ref in zip(B_PATTERNS[1:], strided_refs):
                for rho in range(r):
                    for j in range(4):
                        s_ref[rho, :, col + j * LANES:col + (j + 1) * LANES] = (
                            ybuf[j, pl.ds(rho, tm // r, stride=r), :].astype(BF16))


def _proj_in(x2d, mod, g, w, gain, bd, cos_t, sin_t, T):
    M = x2d.shape[0]
    B = M // T
    tm = 512
    nt = T // tm
    strided = [r for _, r in B_PATTERNS[1:]]
    return pl.pallas_call(
        _proj_in_kernel,
        out_shape=[jax.ShapeDtypeStruct((M, P_WIDTH), BF16)]
        + [jax.ShapeDtypeStruct((B, r, T // r, 1536), BF16) for r in strided],
        grid=(M // tm,),
        in_specs=[
            pl.BlockSpec((tm, D_MODEL), lambda i: (i, 0)),
            pl.BlockSpec((1, N_MOD, D_MODEL), lambda i: (i // nt, 0, 0)),
            pl.BlockSpec((1, D_MODEL), lambda i: (0, 0)),
            pl.BlockSpec((D_MODEL, P_WIDTH), lambda i: (0, 0), pipeline_mode=pl.Buffered(1)),
            pl.BlockSpec((1, P_WIDTH), lambda i: (0, 0)),
            pl.BlockSpec((256, 256), lambda i: (0, 0)),
            pl.BlockSpec((tm, LANES), lambda i: (i % nt, 0)),
            pl.BlockSpec((tm, LANES), lambda i: (i % nt, 0)),
        ],
        out_specs=[pl.BlockSpec((tm, P_WIDTH), lambda i: (i, 0))]
        + [pl.BlockSpec((None, r, tm // r, 1536), lambda i: (i // nt, 0, i % nt, 0)) for r in strided],
        scratch_shapes=[pltpu.VMEM((4, tm, LANES), F32)],
        compiler_params=_cparams(("parallel",)),
        name="proj_in",
    )(x2d, mod, g, w, gain, bd, cos_t, sin_t)


def _banded_kernel(*refs, tq, hw, L, gqa, has_sink, want_lse):
    q_ref, kp_ref, kc_ref, kn_ref, vp_ref, vc_ref, vn_ref, bias_ref = refs[:8]
    rest = refs[8:]
    if has_sink:
        sink_ref, rest = rest[0], rest[1:]
    o_ref = rest[0]
    lse_ref = rest[1] if want_lse else None

    i = pl.program_id(1)
    W = tq + 2 * hw
    q = q_ref[...]
    kwin = jnp.concatenate([kp_ref[...], kc_ref[...], kn_ref[...]], axis=0)
    vwin = jnp.concatenate([vp_ref[...], vc_ref[...], vn_ref[...]], axis=0)
    kpos = i * tq - hw + lax.broadcasted_iota(jnp.int32, (1, W), 1)
    valid = jnp.where(kpos >= 0, kpos, L) < L
    lower = lax.broadcasted_iota(jnp.int32, (1, LANES), 1) < HEAD_DIM
    zero = jnp.zeros((), BF16)
    for jp in range(4):
        qp = q[:, jp * LANES:(jp + 1) * LANES]
        kb = jp // 2 if gqa else jp
        kk = kwin[:, kb * LANES:(kb + 1) * LANES]
        vv = vwin[:, kb * LANES:(kb + 1) * LANES]
        o_pair = None
        lse_pair = None
        for e in range(2):
            h = 2 * jp + e
            sel = lower if e == 0 else jnp.logical_not(lower)
            qe = jnp.where(sel, qp, zero)
            s = lax.dot_general(qe, kk, (((1,), (1,)), ((), ())), preferred_element_type=F32)
            s = jnp.where(valid, s + bias_ref[h], NEG_INF)
            m = jnp.max(s, axis=1, keepdims=True)
            if has_sink:
                sk = sink_ref[h]
                m = jnp.maximum(m, sk)
            p = jnp.exp(s - m)
            d = jnp.sum(p, axis=1, keepdims=True)
            if has_sink:
                d = d + jnp.exp(sk - m)
            ve = jnp.where(sel, vv, zero)
            oe = jnp.dot(p.astype(BF16), ve, preferred_element_type=F32) / d
            o_pair = oe if o_pair is None else o_pair + oe
            if want_lse:
                lse_e = jnp.broadcast_to(m + jnp.log(d), (tq, LANES))
                lse_pair = lse_e if lse_pair is None else jnp.where(lower, lse_pair, lse_e)
        o_ref[:, jp * LANES:(jp + 1) * LANES] = o_pair.astype(o_ref.dtype)
        if want_lse:
            lse_ref[:, jp * LANES:(jp + 1) * LANES] = lse_pair


def _banded_attention(arr, nseq, L, hw, col_q, col_k, col_v, gqa, bias, sink, out_dtype, want_lse, name):
    tq = min(256, L)
    kw = 256 if gqa else 512
    nq = L // tq
    sub = tq // hw
    nh = L // hw

    def cur_map(col, width):
        return lambda n, i: (n * nq + i, col // width)

    def prev_map(col):
        return lambda n, i: (n * nh + jnp.maximum(i * sub - 1, 0), col // kw)

    def next_map(col):
        return lambda n, i: (n * nh + jnp.minimum((i + 1) * sub, nh - 1), col // kw)

    in_specs = [
        pl.BlockSpec((tq, 512), cur_map(col_q, 512)),
        pl.BlockSpec((hw, kw), prev_map(col_k)),
        pl.BlockSpec((tq, kw), cur_map(col_k, kw)),
        pl.BlockSpec((hw, kw), next_map(col_k)),
        pl.BlockSpec((hw, kw), prev_map(col_v)),
        pl.BlockSpec((tq, kw), cur_map(col_v, kw)),
        pl.BlockSpec((hw, kw), next_map(col_v)),
        pl.BlockSpec(bias.shape, lambda n, i: (0, 0, 0)),
    ]
    args = [arr] * 7 + [bias]
    if sink is not None:
        in_specs.append(pl.BlockSpec(memory_space=pltpu.SMEM))
        args.append(sink)
    o_spec = pl.BlockSpec((tq, 512), lambda n, i: (n * nq + i, 0))
    out_shape = [jax.ShapeDtypeStruct((nseq * L, 512), out_dtype)]
    out_specs = [o_spec]
    if want_lse:
        out_shape.append(jax.ShapeDtypeStruct((nseq * L, 512), F32))
        out_specs.append(o_spec)
    return pl.pallas_call(
        functools.partial(_banded_kernel, tq=tq, hw=hw, L=L, gqa=gqa, has_sink=sink is not None,
                          want_lse=want_lse),
        out_shape=out_shape,
        grid=(nseq, nq),
        in_specs=in_specs,
        out_specs=out_specs,
        compiler_params=_cparams(("parallel", "parallel")),
        name=name,
    )(*args)


def _combine_kernel(o1_ref, l1_ref, *rest):
    n_str = len(B_PATTERNS) - 1
    strided_refs, out_ref, scr = rest[:2 * n_str], rest[2 * n_str], rest[2 * n_str + 1:]
    tm = out_ref.shape[0]
    os_, ls_ = [o1_ref[...]], [l1_ref[...]]
    for k, (_, r) in enumerate(B_PATTERNS[1:]):
        for j in range(2):
            src, dst = strided_refs[2 * k + j], scr[2 * k + j]
            for rho in range(r):
                for q in range(4):
                    dst[q, pl.ds(rho, tm // r, stride=r), :] = src[rho, :, q * LANES:(q + 1) * LANES]
        os_.append(jnp.concatenate([scr[2 * k][q] for q in range(4)], axis=1))
        ls_.append(jnp.concatenate([scr[2 * k + 1][q] for q in range(4)], axis=1))
    m = functools.reduce(jnp.maximum, ls_)
    es = [jnp.exp(l - m) for l in ls_]
    acc = sum(e * o for e, o in zip(es, os_))
    out_ref[...] = (acc / sum(es)).astype(out_ref.dtype)


def _combine(o1, l1, strided, B, T):
    M = B * T
    tm = 512
    nt = T // tm
    spec = pl.BlockSpec((tm, 512), lambda i: (i, 0))
    in_specs, args = [spec, spec], [o1, l1]
    for (_, r), pair in zip(B_PATTERNS[1:], strided):
        s_spec = pl.BlockSpec((None, r, tm // r, 512), lambda i: (i // nt, 0, i % nt, 0))
        in_specs += [s_spec, s_spec]
        args += list(pair)
    return pl.pallas_call(
        _combine_kernel,
        out_shape=jax.ShapeDtypeStruct((M, 512), BF16),
        grid=(M // tm,),
        in_specs=in_specs,
        out_specs=spec,
        scratch_shapes=[pltpu.VMEM((4, tm, LANES), F32)] * (2 * (len(B_PATTERNS) - 1)),
        compiler_params=_cparams(("parallel",)),
        name="dilated_combine",
    )(*args)


def _flash_step(s, v_ext, m_ref, acc_ref, tk):
    m_prev = m_ref[...]
    m_next = jnp.maximum(m_prev, jnp.max(s, axis=1, keepdims=True))
    p = jnp.exp2(s - jnp.concatenate([m_next] * (tk // LANES), axis=1))
    alpha = jnp.exp2(m_prev - m_next)
    alpha = jnp.concatenate([alpha] * (acc_ref.shape[-1] // LANES), axis=1)
    acc_ref[...] = alpha * acc_ref[...] + jnp.dot(p.astype(BF16), v_ext, preferred_element_type=F32)
    m_ref[...] = m_next


def _split_pair(qp, lower):
    zero = jnp.zeros((), BF16)
    return jnp.where(lower, qp, zero), jnp.where(lower, zero, qp)


def _dense_gqa_kernel(q_ref, k_ref, v_ref, o_ref, lhs_scr, m_scr, acc_scr, *, tq, tk, T):
    lower = lax.broadcasted_iota(jnp.int32, (1, LANES), 1) < HEAD_DIM
    for g in range(2):
        parts = []
        for jp in (2 * g, 2 * g + 1):
            parts.extend(_split_pair(q_ref[:, jp * LANES:(jp + 1) * LANES], lower))
        lhs_scr[g] = jnp.concatenate(parts, axis=0)
    m_scr[...] = jnp.full(m_scr.shape, NEG_INF, F32)
    acc_scr[...] = jnp.zeros(acc_scr.shape, F32)

    def body(c, carry):
        off = pl.multiple_of(c * tk, tk)
        for g in range(2):
            kc = k_ref[pl.ds(off, tk), g * LANES:(g + 1) * LANES]
            vc = v_ref[pl.ds(off, tk), g * LANES:(g + 1) * LANES]
            s = lax.dot_general(lhs_scr[g], kc, (((1,), (1,)), ((), ())), preferred_element_type=F32)
            _flash_step(s, vc, m_scr.at[g], acc_scr.at[g], tk)
        return carry

    lax.fori_loop(0, T // tk, body, 0)
    for g in range(2):
        for jj in range(2):
            a_even = acc_scr[g, (2 * jj) * tq:(2 * jj + 1) * tq, :]
            a_odd = acc_scr[g, (2 * jj + 1) * tq:(2 * jj + 2) * tq, :]
            den_even = jnp.where(lower, pltpu.roll(a_even, HEAD_DIM, 1), a_even)
            den_odd = jnp.where(lower, pltpu.roll(a_odd, HEAD_DIM, 1), a_odd)
            out = jnp.where(lower, a_even / den_even, pltpu.roll(a_odd, HEAD_DIM, 1) / den_odd)
            jp = 2 * g + jj
            o_ref[:, jp * LANES:(jp + 1) * LANES] = out.astype(o_ref.dtype)


def _dense_gqa(P, B, T):
    tq = 128
    tk = min(1024, T)
    nq = T // tq
    rows = 4 * tq
    return pl.pallas_call(
        functools.partial(_dense_gqa_kernel, tq=tq, tk=tk, T=T),
        out_shape=jax.ShapeDtypeStruct((B * T, 512), BF16),
        grid=(B, nq),
        in_specs=[
            pl.BlockSpec((tq, 512), lambda b, i: (b * nq + i, COL_QD // 512)),
            pl.BlockSpec((T, 256), lambda b, i: (b, COL_KD // 256)),
            pl.BlockSpec((T, 256), lambda b, i: (b, COL_VD // 256)),
        ],
        out_specs=pl.BlockSpec((tq, 512), lambda b, i: (b * nq + i, 0)),
        scratch_shapes=[pltpu.VMEM((2, rows, LANES), BF16), pltpu.VMEM((2, rows, LANES), F32),
                        pltpu.VMEM((2, rows, LANES), F32)],
        compiler_params=_cparams(("parallel", "parallel")),
        name="dense_gqa",
    )(P, P, P)


def _diff_kernel(q_ref, k_ref, v_ref, bias_ref, lamv_ref, subln_ref, o_ref, lhs_scr, m_scr, acc_scr, *,
                 tq, tk, T, lam_init):
    i = pl.program_id(2)
    lower = lax.broadcasted_iota(jnp.int32, (1, LANES), 1) < HEAD_DIM
    for e in range(2):
        pair = _split_pair(q_ref[:, e * LANES:(e + 1) * LANES], lower)
        lhs_scr[e] = jnp.concatenate(pair, axis=0)
    m_scr[...] = jnp.full(m_scr.shape, NEG_INF, F32)
    acc_scr[...] = jnp.zeros(acc_scr.shape, F32)
    nt = tk // LANES
    nrb = tq // LANES
    ones = jnp.ones((tk, LANES), BF16)

    def body(c, carry):
        off = pl.multiple_of(c * tk, tk)
        for e in range(2):
            kc = k_ref[pl.ds(off, tk), e * LANES:(e + 1) * LANES]
            v_ext = jnp.concatenate([v_ref[pl.ds(off, tk), e * LANES:(e + 1) * LANES], ones], axis=1)
            s = lax.dot_general(lhs_scr[e], kc, (((1,), (1,)), ((), ())), preferred_element_type=F32)
            rows = []
            for rb in range(nrb):
                tiles = []
                for t in range(nt):
                    delta = jnp.clip(c * nt + t - (i * nrb + rb), -C_BIAS_REACH, C_BIAS_REACH)
                    tiles.append(bias_ref[e, delta + C_BIAS_REACH])
                rows.append(jnp.concatenate(tiles, axis=1))
            b = jnp.concatenate(rows, axis=0)
            s = s + jnp.concatenate([b, b], axis=0)
            _flash_step(s, v_ext, m_scr.at[e], acc_scr.at[e], tk)
        return carry

    lax.fori_loop(0, T // tk, body, 0)
    lv = lamv_ref[...]
    lam = (jnp.exp(jnp.sum(lv[0:1] * lv[1:2], axis=1, keepdims=True))
           - jnp.exp(jnp.sum(lv[2:3] * lv[3:4], axis=1, keepdims=True)) + lam_init)
    for e in range(2):
        o = acc_scr[e, :, :LANES] / acc_scr[e, :, LANES:]
        o = o[:tq] - lam * o[tq:]
        y = o * lax.rsqrt(jnp.mean(o * o, axis=-1, keepdims=True) + EPS) * subln_ref[...]
        o_ref[:, e * LANES:(e + 1) * LANES] = (y * (1.0 - lam_init)).astype(o_ref.dtype)


def _diff_attention(P, B, T, bias_c, lamv, subln, lam_init):
    tq = 256
    tk = min(1024, T)
    nq = T // tq
    rows = 2 * tq
    return pl.pallas_call(
        functools.partial(_diff_kernel, tq=tq, tk=tk, T=T, lam_init=lam_init),
        out_shape=jax.ShapeDtypeStruct((B * T, 512), BF16),
        grid=(B, C_HEADS // 2, nq),
        in_specs=[
            pl.BlockSpec((tq, 256), lambda b, h, i: (b * nq + i, COL_QC // 256 + h)),
            pl.BlockSpec((T, 256), lambda b, h, i: (b, COL_KC // 256 + h)),
            pl.BlockSpec((T, 256), lambda b, h, i: (b, COL_VC // 256 + h)),
            pl.BlockSpec((2, 2 * C_BIAS_REACH + 1, LANES, LANES), lambda b, h, i: (h, 0, 0, 0)),
            pl.BlockSpec((4, HEAD_DIM), lambda b, h, i: (0, 0)),
            pl.BlockSpec((1, C_VDIM), lambda b, h, i: (0, 0)),
        ],
        out_specs=pl.BlockSpec((tq, 256), lambda b, h, i: (b * nq + i, h)),
        scratch_shapes=[pltpu.VMEM((2, rows, LANES), BF16), pltpu.VMEM((2, rows, LANES), F32),
                        pltpu.VMEM((2, rows, 2 * LANES), F32)],
        compiler_params=_cparams(("parallel", "parallel", "parallel")),
        name="diff_attn",
    )(P, P, P, bias_c, lamv, subln)


def _gated_out_kernel(*refs, n_a, gate_row):
    a_refs = refs[:n_a]
    w_ref, x_ref, mod_ref, o_ref = refs[n_a:]
    a = a_refs[0][...] if n_a == 1 else jnp.concatenate([r[...] for r in a_refs], axis=1)
    y = jnp.dot(a, w_ref[...], preferred_element_type=F32)
    o_ref[...] = x_ref[...] + mod_ref[0, gate_row:gate_row + 1, :] * y


def _gated_out(a_list, w, x2d, mod, gate_row, T):
    M = x2d.shape[0]
    tm, tn = 1024, 512
    nt = T // tm
    a_specs = [pl.BlockSpec((tm, a.shape[1]), lambda i, j: (i, 0)) for a in a_list]
    return pl.pallas_call(
        functools.partial(_gated_out_kernel, n_a=len(a_list), gate_row=gate_row),
        out_shape=jax.ShapeDtypeStruct((M, D_MODEL), F32),
        grid=(M // tm, D_MODEL // tn),
        in_specs=a_specs + [
            pl.BlockSpec((w.shape[0], tn), lambda i, j: (0, j)),
            pl.BlockSpec((tm, tn), lambda i, j: (i, j)),
            pl.BlockSpec((1, N_MOD, tn), lambda i, j: (i // nt, 0, j)),
        ],
        out_specs=pl.BlockSpec((tm, tn), lambda i, j: (i, j)),
        compiler_params=_cparams(("parallel", "arbitrary")),
        name=f"gated_out_{gate_row}",
    )(*a_list, w, x2d, mod)


def _ffn_up_kernel(x_ref, mod_ref, g_ref, wg_ref, wu_ref, o_ref, h_scr):
    @pl.when(pl.program_id(1) == 0)
    def _():
        mod = mod_ref[0]
        h_scr[...] = _modulated_norm(x_ref[...], g_ref[...], mod[3:4], mod[4:5]).astype(BF16)

    hb = h_scr[...]
    gate = jnp.dot(hb, wg_ref[...], preferred_element_type=F32)
    up = jnp.dot(hb, wu_ref[...], preferred_element_type=F32)
    o_ref[...] = (gate * jax.nn.sigmoid(gate) * up).astype(o_ref.dtype)


def _ffn_up(x2d, mod, g, w_gu, T):
    M = x2d.shape[0]
    tm, tn = 1024, 512
    nt = T // tm
    nj = FF_DIM // tn
    return pl.pallas_call(
        _ffn_up_kernel,
        out_shape=jax.ShapeDtypeStruct((M, FF_DIM), BF16),
        grid=(M // tm, nj),
        in_specs=[
            pl.BlockSpec((tm, D_MODEL), lambda i, j: (i, 0)),
            pl.BlockSpec((1, N_MOD, D_MODEL), lambda i, j: (i // nt, 0, 0)),
            pl.BlockSpec((1, D_MODEL), lambda i, j: (0, 0)),
            pl.BlockSpec((D_MODEL, tn), lambda i, j: (0, j)),
            pl.BlockSpec((D_MODEL, tn), lambda i, j: (0, j + nj)),
        ],
        out_specs=pl.BlockSpec((tm, tn), lambda i, j: (i, j)),
        scratch_shapes=[pltpu.VMEM((tm, D_MODEL), BF16)],
        compiler_params=_cparams(("parallel", "arbitrary")),
        name="ffn_up",
    )(x2d, mod, g, w_gu, w_gu)


def _relpos_bucket(rel):
    half = NUM_BUCKETS // 2
    max_exact = half // 2
    n = jnp.abs(rel)
    nf = jnp.maximum(n, 1).astype(F32)
    large = max_exact + (jnp.log(nf / max_exact) / math.log(RELPOS_MAX_DIST / max_exact)
                         * (half - max_exact)).astype(jnp.int32)
    large = jnp.minimum(large, half - 1)
    return jnp.where(rel > 0, half, 0) + jnp.where(n < max_exact, n, large)


def _relpos_bias(table, rel):
    bucket = _relpos_bucket(rel)[None]
    tab = table.astype(F32).reshape((NUM_BUCKETS, -1) + (1,) * rel.ndim)
    out = jnp.zeros((table.shape[1],) + rel.shape, F32)
    for b in range(NUM_BUCKETS):
        out = jnp.where(bucket == b, tab[b], out)
    return out


def _band_bias(table, tq, hw, stride):
    rel = jnp.arange(tq + 2 * hw)[None, :] - hw - jnp.arange(tq)[:, None]
    return jnp.where((jnp.abs(rel) <= hw)[None], _relpos_bias(table, rel * stride), NEG_INF)


def _dense_bias(table, blk):
    delta = jnp.arange(-C_BIAS_REACH, C_BIAS_REACH + 1)
    rel = delta[:, None, None] * blk + jnp.arange(blk)[None, None, :] - jnp.arange(blk)[None, :, None]
    return _relpos_bias(table, rel) * LOG2E


def _rope_tables(T):
    t = jnp.arange(T)
    pos = jnp.stack([t // GRID_W, t % GRID_W], axis=-1).astype(F32)
    inv_freq = ROPE_THETA ** (-jnp.arange(ROPE_PAIRS, dtype=F32) / ROPE_PAIRS)
    ang = pos[:, :, None] * inv_freq
    d = np.arange(LANES) % HEAD_DIM
    axis, half, pair = d // 32, (d % 32) // 16, d % 16
    cos_t = jnp.cos(ang)[:, axis, pair]
    sin_t = jnp.sin(ang)[:, axis, pair] * jnp.asarray(np.where(half == 0, -1.0, 1.0), F32)
    return cos_t, sin_t


def _dup_kv(w, off, pad=False):
    h0, h1 = w[..., off:off + 64], w[..., off + 64:off + 128]
    return [h0, jnp.zeros_like(h0), h1, jnp.zeros_like(h1)] if pad else [h0, h0, h1, h1]


def _prep_w_in(w_in):
    s = lambda a, b: w_in[..., a:b]
    cols = [s(0, 512), s(768, 1280), s(2304, 2816), s(3840, 4352),
            s(1280, 1792), s(2816, 3328), s(1792, 2304), s(3328, 3840)]
    cols += _dup_kv(w_in, 512) + _dup_kv(w_in, 4352) + _dup_kv(w_in, 640) + _dup_kv(w_in, 4480, pad=True)
    return jnp.concatenate(cols, axis=-1).astype(BF16)


def _prep_gain(qk_gain):
    qs = HEAD_DIM ** -0.5
    t = lambda g, n: jnp.tile(g, (1, n))
    ones = jnp.ones((DEPTH, 512), F32)
    rows = [t(qk_gain[:, 0, 0], 8) * qs, t(qk_gain[:, 1, 0], 8) * qs, t(qk_gain[:, 2, 0], 8) * (qs * LOG2E),
            t(qk_gain[:, 3, 0], 8) * (qs * LOG2E), t(qk_gain[:, 1, 1], 8), t(qk_gain[:, 2, 1], 8), ones, ones,
            t(qk_gain[:, 0, 1], 4), t(qk_gain[:, 3, 1], 4), ones]
    return jnp.concatenate(rows, axis=-1).reshape(DEPTH, 1, P_WIDTH)


def _trunk(x, mods, consts, weights):
    B, T, _ = x.shape
    M = B * T
    (norm_mix, norm_ffn, w_in_p, gain, w_out, sink_a, diff_lambda, diff_subln, w_gu, w_down) = weights
    bd, bias_a, bias_b, bias_c = consts
    cos_t, sin_t = _rope_tables(T)
    x2d = x.reshape(M, D_MODEL)
    for l in range(DEPTH):
        mod = mods[l]
        P, *P_strided = _proj_in(x2d, mod, norm_mix[l][None], w_in_p[l], gain[l], bd, cos_t, sin_t, T)
        (oa,) = _banded_attention(P, B, T, A_HALF_WIN, COL_QA, COL_KA, COL_VA, True, bias_a, sink_a[l],
                                  BF16, False, "banded_a")
        window, _ = B_PATTERNS[0]
        o1, l1 = _banded_attention(P, B, T, window // 2, COL_QB, COL_KB, COL_VB, False, bias_b[0], None,
                                   F32, True, "banded_b1")
        strided = []
        for (window, r), bias, Pr in zip(B_PATTERNS[1:], bias_b[1:], P_strided):
            L = T // r
            o, lse = _banded_attention(Pr.reshape(B * r * L, 1536), B * r, L, window // (2 * r), 0, 512, 1024,
                                       False, bias, None, F32, True, f"banded_b{r}")
            strided.append((o.reshape(B, r, L, 512), lse.reshape(B, r, L, 512)))
        ob = _combine(o1, l1, strided, B, T)
        lam_init = 0.8 - 0.6 * math.exp(-0.3 * l)
        oc = _diff_attention(P, B, T, bias_c, diff_lambda[l], diff_subln[l][None], lam_init)
        od = _dense_gqa(P, B, T)
        x2d = _gated_out([oa, ob, oc, od], w_out[l], x2d, mod, 2, T)
        hf = _ffn_up(x2d, mod, norm_ffn[l][None], w_gu[l], T)
        x2d = _gated_out([hf], w_down[l], x2d, mod, 5, T)
    return x2d.reshape(B, T, D_MODEL)


def kernel(x_prompt, x_sample, c_prompt, c_sample, w_mod, b_mod, norm_mix, norm_ffn, w_in, w_out, qk_gain,
           sink_a, relpos_table, diff_lambda, diff_subln, w_gate_up, w_down):
    nb = x_prompt.shape[0]
    ns = x_sample.shape[0]
    c_all = jnp.concatenate([c_prompt, c_sample, jnp.zeros((8 - nb - ns, D_MODEL), F32)], axis=0)
    mods = _mod_all(c_all, w_mod, b_mod).reshape(DEPTH, 8, N_MOD, D_MODEL)

    idx = np.arange(256)
    bd = jnp.asarray((idx[:, None] // HEAD_DIM == idx[None, :] // HEAD_DIM) / HEAD_DIM, BF16)
    tab_a = relpos_table[:, :A_HEADS]
    tab_b = relpos_table[:, A_HEADS:A_HEADS + B_HEADS]
    tab_c = relpos_table[:, A_HEADS + B_HEADS:]
    weights = (norm_mix, norm_ffn, _prep_w_in(w_in), _prep_gain(qk_gain), w_out.astype(BF16), sink_a,
               diff_lambda, diff_subln, w_gate_up.astype(BF16), w_down.astype(BF16))

    outs = []
    for x, lo, hi in ((x_prompt, 0, nb), (x_sample, nb, nb + ns)):
        T = x.shape[1]
        bias_a = _band_bias(tab_a, min(256, T), A_HALF_WIN, 1)
        bias_b = [_band_bias(tab_b, min(256, T // r), w // (2 * r), r) for w, r in B_PATTERNS]
        bias_c = _dense_bias(tab_c, 128)
        outs.append(_trunk(x, mods[:, lo:hi], (bd, bias_a, bias_b, bias_c), weights))
    return tuple(outs)
```

```python
import functools
import math

import numpy as np
import jax
import jax.numpy as jnp
from jax import lax
from jax.experimental import pallas as pl
from jax.experimental.pallas import tpu as pltpu

D_MODEL = 2048
DEPTH = 4
HEAD_DIM = 64
A_HEADS = 8
A_HALF_WIN = 128
B_HEADS = 8
B_PATTERNS = ((128, 1), (512, 4), (2048, 16))
C_HEADS = 4
C_VDIM = 128
GRID_W = 64
ROPE_THETA = 10000.0
ROPE_PAIRS = HEAD_DIM // 4
NUM_BUCKETS = 32
RELPOS_MAX_DIST = 1024
FF_DIM = 5632
N_MOD = 6
NEG_INF = -1e30
EPS = 1e-6
LOG2E = math.log2(math.e)

LANES = 128
P_WIDTH = 5120
N_SUB = P_WIDTH // 256
SUB_NORM = (True,) * 12 + (False,) * 4 + (True, True, False, False)
SUB_ROPE = (False,) * 6 + (True, True) + (False,) * 8 + (False, True, False, False)
SUB_ONES = (False,) * 18 + (True, True)
B_CHUNKS = (1, 4, 6)
COL_QA, COL_QB, COL_QC, COL_QD = 0, 512, 1024, 1536
COL_KB, COL_KC, COL_VB, COL_VC = 2048, 2560, 3072, 3584
COL_KA, COL_KD, COL_VA, COL_VD = 4096, 4352, 4608, 4864
C_BIAS_REACH = 6
VMEM_LIMIT = 56 * 1024 * 1024

F32 = jnp.float32
BF16 = jnp.bfloat16


def _cparams(sem):
    return pltpu.CompilerParams(dimension_semantics=sem, vmem_limit_bytes=VMEM_LIMIT)


def _mod_kernel(c_ref, w_ref, b_ref, o_ref):
    c = c_ref[...]
    a = (c * jax.nn.sigmoid(c)).astype(BF16)
    o_ref[0] = jnp.dot(a, w_ref[0].astype(BF16), preferred_element_type=F32) + b_ref[0]


def _mod_all(c_all, w_mod, b_mod):
    tn = 1024
    n = N_MOD * D_MODEL
    return pl.pallas_call(
        _mod_kernel,
        out_shape=jax.ShapeDtypeStruct((DEPTH, 8, n), F32),
        grid=(DEPTH, n // tn),
        in_specs=[
            pl.BlockSpec((8, D_MODEL), lambda l, j: (0, 0)),
            pl.BlockSpec((1, D_MODEL, tn), lambda l, j: (l, 0, j)),
            pl.BlockSpec((1, 1, tn), lambda l, j: (l, 0, j)),
        ],
        out_specs=pl.BlockSpec((1, 8, tn), lambda l, j: (l, 0, j)),
        compiler_params=_cparams(("parallel", "parallel")),
        name="mod_proj",
    )(c_all, w_mod, b_mod.reshape(DEPTH, 1, n))


def _modulated_norm(x, g, shift, scale):
    ms = jnp.mean(x * x, axis=-1, keepdims=True)
    h = x * lax.rsqrt(ms + EPS) * g
    return h * (1.0 + scale) + shift


def _proj_in_kernel(x_ref, mod_ref, g_ref, w_ref, gain_ref, bd_ref, cos_ref, sin_ref, o_ref, *rest):
    strided_refs, ybuf = rest[:-1], rest[-1]
    tm = x_ref.shape[0]
    mod = mod_ref[0]
    hb = _modulated_norm(x_ref[...], g_ref[...], mod[0:1], mod[1:2]).astype(BF16)
    lane = lax.broadcasted_iota(jnp.int32, (1, 256), 1)
    upper16 = (lane & 16) != 0
    bd = bd_ref[...]
    for c in range(P_WIDTH // 512):
        r = jnp.dot(hb, w_ref[:, c * 512:(c + 1) * 512], preferred_element_type=F32)
        for half in range(2):
            sub = 2 * c + half
            y = r[:, half * 256:(half + 1) * 256]
            if SUB_NORM[sub]:
                sq = y * y
                hi = sq.astype(BF16)
                lo = (sq - hi.astype(F32)).astype(BF16)
                msq = (jnp.dot(hi, bd, preferred_element_type=F32)
                       + jnp.dot(lo, bd, preferred_element_type=F32))
                y = y * lax.rsqrt(msq + EPS) * gain_ref[:, sub * 256:(sub + 1) * 256]
            if SUB_ROPE[sub]:
                cs = jnp.concatenate([cos_ref[...]] * 2, axis=1)
                sn = jnp.concatenate([sin_ref[...]] * 2, axis=1)
                partner = jnp.where(upper16, pltpu.roll(y, 16, 1), pltpu.roll(y, 256 - 16, 1))
                y = y * cs + partner * sn
            if SUB_ONES[sub]:
                y = jnp.where((lane & HEAD_DIM) != 0, 1.0, y)
            o_ref[:, sub * 256:(sub + 1) * 256] = y.astype(BF16)
            if c in B_CHUNKS:
                for j in range(2):
                    ybuf[2 * half + j] = y[:, j * LANES:(j + 1) * LANES]
        if c in B_CHUNKS:
            col = B_CHUNKS.index(c) * 512
            for (_, r), s_ref in zip(B_PATTERNS[1:], strided_refs):
                for rho in range(r):
                    for j in range(4):
                        s_ref[rho, :, col + j * LANES:col + (j + 1) * LANES] = (
                            ybuf[j, pl.ds(rho, tm // r, stride=r), :].astype(BF16))


def _proj_in(x2d, mod, g, w, gain, bd, cos_t, sin_t, T):
    M = x2d.shape[0]
    B = M // T
    tm = 512
    nt = T // tm
    strided = [r for _, r in B_PATTERNS[1:]]
    return pl.pallas_call(
        _proj_in_kernel,
        out_shape=[jax.ShapeDtypeStruct((M, P_WIDTH), BF16)]
        + [jax.ShapeDtypeStruct((B, r, T // r, 1536), BF16) for r in strided],
        grid=(M // tm,),
        in_specs=[
            pl.BlockSpec((tm, D_MODEL), lambda i: (i, 0)),
            pl.BlockSpec((1, N_MOD, D_MODEL), lambda i: (i // nt, 0, 0)),
            pl.BlockSpec((1, D_MODEL), lambda i: (0, 0)),
            pl.BlockSpec((D_MODEL, P_WIDTH), lambda i: (0, 0), pipeline_mode=pl.Buffered(1)),
            pl.BlockSpec((1, P_WIDTH), lambda i: (0, 0)),
            pl.BlockSpec((256, 256), lambda i: (0, 0)),
            pl.BlockSpec((tm, LANES), lambda i: (i % nt, 0)),
            pl.BlockSpec((tm, LANES), lambda i: (i % nt, 0)),
        ],
        out_specs=[pl.BlockSpec((tm, P_WIDTH), lambda i: (i, 0))]
        + [pl.BlockSpec((None, r, tm // r, 1536), lambda i: (i // nt, 0, i % nt, 0)) for r in strided],
        scratch_shapes=[pltpu.VMEM((4, tm, LANES), F32)],
        compiler_params=_cparams(("parallel",)),
        name="proj_in",
    )(x2d, mod, g, w, gain, bd, cos_t, sin_t)


_NT_DIMS = (((1,), (1,)), ((), ()))


def _banded_kernel(*refs, tq, sbq, hw, nq, gqa):
    q_ref, kp_ref, kc_ref, kn_ref, vp_ref, vc_ref, vn_ref, bias_ref = refs[:8]
    if gqa:
        sink_ref, o_ref = refs[8:]
    else:
        o_ref, lse_ref = refs[8:]

    i = pl.program_id(1)
    nsb = tq // sbq
    wsb = sbq + 2 * hw
    kwin = jnp.concatenate([kp_ref[...], kc_ref[...], kn_ref[...]], axis=0)
    vwin = jnp.concatenate([vp_ref[...], vc_ref[...], vn_ref[...]], axis=0)
    lower = lax.broadcasted_iota(jnp.int32, (1, LANES), 1) < HEAD_DIM
    zero = jnp.zeros((), BF16)
    for sb in range(nsb):
        variant = 0
        if sb == 0:
            variant = variant + jnp.where(i == 0, 1, 0)
        if sb == nsb - 1:
            variant = variant + jnp.where(i == nq - 1, 2, 0)
        rows = slice(sb * sbq, (sb + 1) * sbq)
        k_sb = kwin[sb * sbq:sb * sbq + wsb]
        v_sb = vwin[sb * sbq:sb * sbq + wsb]
        if gqa:
            for g in range(2):
                kk = k_sb[:, g * LANES:(g + 1) * LANES]
                vv = v_sb[:, g * LANES:(g + 1) * LANES]
                parts = []
                for jp in (2 * g, 2 * g + 1):
                    parts.extend(_split_pair(q_ref[rows, jp * LANES:(jp + 1) * LANES], lower))
                s_all = lax.dot_general(jnp.concatenate(parts, axis=0), kk, _NT_DIMS, preferred_element_type=F32)
                ps, ms = [], []
                for hh in range(4):
                    h = 4 * g + hh
                    s = s_all[hh * sbq:(hh + 1) * sbq] + bias_ref[variant, h]
                    m = jnp.maximum(jnp.max(s, axis=1, keepdims=True), sink_ref[h])
                    ps.append(jnp.exp2(s - m).astype(BF16))
                    ms.append(m)
                acc_all = jnp.dot(jnp.concatenate(ps, axis=0), vv, preferred_element_type=F32)
                for jj in range(2):
                    outs = []
                    for e in range(2):
                        hh = 2 * jj + e
                        acc = acc_all[hh * sbq:(hh + 1) * sbq]
                        swapped = pltpu.roll(acc, HEAD_DIM, 1)
                        den = jnp.where(lower, swapped, acc) + jnp.exp2(sink_ref[4 * g + hh] - ms[hh])
                        outs.append((acc if e == 0 else swapped) / den)
                    jp = 2 * g + jj
                    o_ref[rows, jp * LANES:(jp + 1) * LANES] = jnp.where(lower, outs[0], outs[1]).astype(o_ref.dtype)
        else:
            for jp in range(4):
                qp = q_ref[rows, jp * LANES:(jp + 1) * LANES]
                kk = k_sb[:, jp * LANES:(jp + 1) * LANES]
                vv = v_sb[:, jp * LANES:(jp + 1) * LANES]
                o_pair = None
                lse_pair = None
                for e, qe in enumerate(_split_pair(qp, lower)):
                    s = lax.dot_general(qe, kk, _NT_DIMS, preferred_element_type=F32) + bias_ref[variant, 2 * jp + e]
                    m = jnp.max(s, axis=1, keepdims=True)
                    p = jnp.exp2(s - m)
                    d = jnp.sum(p, axis=1, keepdims=True)
                    ve = jnp.where(lower if e == 0 else jnp.logical_not(lower), vv, zero)
                    oe = jnp.dot(p.astype(BF16), ve, preferred_element_type=F32) / d
                    lse_e = jnp.broadcast_to(m + jnp.log2(d), (sbq, LANES))
                    o_pair = oe if e == 0 else o_pair + oe
                    lse_pair = lse_e if e == 0 else jnp.where(lower, lse_pair, lse_e)
                o_ref[rows, jp * LANES:(jp + 1) * LANES] = o_pair.astype(o_ref.dtype)
                lse_ref[rows, jp * LANES:(jp + 1) * LANES] = lse_pair


def _banded_attention(arr, nseq, L, hw, col_q, col_k, col_v, gqa, bias, sink, name):
    tq = min(256, L)
    sbq = bias.shape[2]
    kw = 256 if gqa else 512
    nq = L // tq
    sub = tq // hw
    nh = L // hw

    def cur_map(col, width):
        return lambda n, i: (n * nq + i, col // width)

    def prev_map(col):
        return lambda n, i: (n * nh + jnp.maximum(i * sub - 1, 0), col // kw)

    def next_map(col):
        return lambda n, i: (n * nh + jnp.minimum((i + 1) * sub, nh - 1), col // kw)

    in_specs = [
        pl.BlockSpec((tq, 512), cur_map(col_q, 512)),
        pl.BlockSpec((hw, kw), prev_map(col_k)),
        pl.BlockSpec((tq, kw), cur_map(col_k, kw)),
        pl.BlockSpec((hw, kw), next_map(col_k)),
        pl.BlockSpec((hw, kw), prev_map(col_v)),
        pl.BlockSpec((tq, kw), cur_map(col_v, kw)),
        pl.BlockSpec((hw, kw), next_map(col_v)),
        pl.BlockSpec(bias.shape, lambda n, i: (0, 0, 0, 0)),
    ]
    args = [arr] * 7 + [bias]
    o_spec = pl.BlockSpec((tq, 512), lambda n, i: (n * nq + i, 0))
    if gqa:
        in_specs.append(pl.BlockSpec(memory_space=pltpu.SMEM))
        args.append(sink)
        out_shape = [jax.ShapeDtypeStruct((nseq * L, 512), BF16)]
        out_specs = [o_spec]
    else:
        out_shape = [jax.ShapeDtypeStruct((nseq * L, 512), F32)] * 2
        out_specs = [o_spec, o_spec]
    return pl.pallas_call(
        functools.partial(_banded_kernel, tq=tq, sbq=sbq, hw=hw, nq=nq, gqa=gqa),
        out_shape=out_shape,
        grid=(nseq, nq),
        in_specs=in_specs,
        out_specs=out_specs,
        compiler_params=_cparams(("parallel", "parallel")),
        name=name,
    )(*args)


def _combine_kernel(o1_ref, l1_ref, *rest):
    n_str = len(B_PATTERNS) - 1
    strided_refs, out_ref, scr = rest[:2 * n_str], rest[2 * n_str], rest[2 * n_str + 1:]
    tm = out_ref.shape[0]
    os_, ls_ = [o1_ref[...]], [l1_ref[...]]
    for k, (_, r) in enumerate(B_PATTERNS[1:]):
        for j in range(2):
            src, dst = strided_refs[2 * k + j], scr[2 * k + j]
            for rho in range(r):
                for q in range(4):
                    dst[q, pl.ds(rho, tm // r, stride=r), :] = src[rho, :, q * LANES:(q + 1) * LANES]
        os_.append(jnp.concatenate([scr[2 * k][q] for q in range(4)], axis=1))
        ls_.append(jnp.concatenate([scr[2 * k + 1][q] for q in range(4)], axis=1))
    m = functools.reduce(jnp.maximum, ls_)
    es = [jnp.exp2(l - m) for l in ls_]
    acc = sum(e * o for e, o in zip(es, os_))
    out_ref[...] = (acc / sum(es)).astype(out_ref.dtype)


def _combine(o1, l1, strided, B, T):
    M = B * T
    tm = 512
    nt = T // tm
    spec = pl.BlockSpec((tm, 512), lambda i: (i, 0))
    in_specs, args = [spec, spec], [o1, l1]
    for (_, r), pair in zip(B_PATTERNS[1:], strided):
        s_spec = pl.BlockSpec((None, r, tm // r, 512), lambda i: (i // nt, 0, i % nt, 0))
        in_specs += [s_spec, s_spec]
        args += list(pair)
    return pl.pallas_call(
        _combine_kernel,
        out_shape=jax.ShapeDtypeStruct((M, 512), BF16),
        grid=(M // tm,),
        in_specs=in_specs,
        out_specs=spec,
        scratch_shapes=[pltpu.VMEM((4, tm, LANES), F32)] * (2 * (len(B_PATTERNS) - 1)),
        compiler_params=_cparams(("parallel",)),
        name="dilated_combine",
    )(*args)


def _flash_step(s, v_ext, m_ref, acc_ref, tk):
    m_prev = m_ref[...]
    m_next = jnp.maximum(m_prev, jnp.max(s, axis=1, keepdims=True))
    p = jnp.exp2(s - jnp.concatenate([m_next] * (tk // LANES), axis=1))
    alpha = jnp.exp2(m_prev - m_next)
    alpha = jnp.concatenate([alpha] * (acc_ref.shape[-1] // LANES), axis=1)
    acc_ref[...] = alpha * acc_ref[...] + jnp.dot(p.astype(BF16), v_ext, preferred_element_type=F32)
    m_ref[...] = m_next


def _split_pair(qp, lower):
    zero = jnp.zeros((), BF16)
    return jnp.where(lower, qp, zero), jnp.where(lower, zero, qp)


def _chunk_rows(c, tk):
    return pl.ds(c * tk if isinstance(c, int) else pl.multiple_of(c * tk, tk), tk)


def _pipelined_chunks(n, scores, consume):
    scores(0, 0)

    def pair(c, last):
        scores(c + 1, 1)
        consume(c, 0)
        if not last:
            scores(c + 2, 0)
        consume(c + 1, 1)

    def body(j, carry):
        pair(2 * j, False)
        return carry

    lax.fori_loop(0, n // 2 - 1, body, 0)
    pair(n - 2, True)


def _dense_gqa_kernel(q_ref, k_ref, v_ref, o_ref, lhs_scr, m_scr, acc_scr, s_scr, *, tq, tk, T):
    lower = lax.broadcasted_iota(jnp.int32, (1, LANES), 1) < HEAD_DIM
    for g in range(2):
        parts = []
        for jp in (2 * g, 2 * g + 1):
            parts.extend(_split_pair(q_ref[:, jp * LANES:(jp + 1) * LANES], lower))
        lhs_scr[g] = jnp.concatenate(parts, axis=0)
    m_scr[...] = jnp.full(m_scr.shape, NEG_INF, F32)
    acc_scr[...] = jnp.zeros(acc_scr.shape, F32)

    def scores(c, slot):
        for g in range(2):
            kc = k_ref[_chunk_rows(c, tk), g * LANES:(g + 1) * LANES]
            s_scr[slot, g] = lax.dot_general(lhs_scr[g], kc, _NT_DIMS, preferred_element_type=F32)

    def consume(c, slot):
        for g in range(2):
            vc = v_ref[_chunk_rows(c, tk), g * LANES:(g + 1) * LANES]
            _flash_step(s_scr[slot, g], vc, m_scr.at[g], acc_scr.at[g], tk)

    _pipelined_chunks(T // tk, scores, consume)
    for g in range(2):
        for jj in range(2):
            a_even = acc_scr[g, (2 * jj) * tq:(2 * jj + 1) * tq, :]
            a_odd = acc_scr[g, (2 * jj + 1) * tq:(2 * jj + 2) * tq, :]
            den_even = jnp.where(lower, pltpu.roll(a_even, HEAD_DIM, 1), a_even)
            den_odd = jnp.where(lower, pltpu.roll(a_odd, HEAD_DIM, 1), a_odd)
            out = jnp.where(lower, a_even / den_even, pltpu.roll(a_odd, HEAD_DIM, 1) / den_odd)
            jp = 2 * g + jj
            o_ref[:, jp * LANES:(jp + 1) * LANES] = out.astype(o_ref.dtype)


def _dense_gqa(P, B, T):
    tq = 128
    tk = min(1024, T // 2)
    nq = T // tq
    rows = 4 * tq
    return pl.pallas_call(
        functools.partial(_dense_gqa_kernel, tq=tq, tk=tk, T=T),
        out_shape=jax.ShapeDtypeStruct((B * T, 512), BF16),
        grid=(B, nq),
        in_specs=[
            pl.BlockSpec((tq, 512), lambda b, i: (b * nq + i, COL_QD // 512)),
            pl.BlockSpec((T, 256), lambda b, i: (b, COL_KD // 256)),
            pl.BlockSpec((T, 256), lambda b, i: (b, COL_VD // 256)),
        ],
        out_specs=pl.BlockSpec((tq, 512), lambda b, i: (b * nq + i, 0)),
        scratch_shapes=[pltpu.VMEM((2, rows, LANES), BF16), pltpu.VMEM((2, rows, LANES), F32),
                        pltpu.VMEM((2, rows, LANES), F32), pltpu.VMEM((2, 2, rows, tk), F32)],
        compiler_params=_cparams(("parallel", "parallel")),
        name="dense_gqa",
    )(P, P, P)


def _diff_kernel(q_ref, k_ref, v_ref, bias_ref, lamv_ref, subln_ref, o_ref, lhs_scr, m_scr, acc_scr, s_scr, *,
                 tq, tk, T, lam_init):
    i = pl.program_id(2)
    lower = lax.broadcasted_iota(jnp.int32, (1, LANES), 1) < HEAD_DIM
    for e in range(2):
        pair = _split_pair(q_ref[:, e * LANES:(e + 1) * LANES], lower)
        lhs_scr[e] = jnp.concatenate(pair, axis=0)
    m_scr[...] = jnp.full(m_scr.shape, NEG_INF, F32)
    acc_scr[...] = jnp.zeros(acc_scr.shape, F32)
    nt = tk // LANES
    nrb = tq // LANES
    ones = jnp.ones((tk, LANES), BF16)

    def scores(c, slot):
        for e in range(2):
            kc = k_ref[_chunk_rows(c, tk), e * LANES:(e + 1) * LANES]
            s = lax.dot_general(lhs_scr[e], kc, _NT_DIMS, preferred_element_type=F32)
            rows = []
            for rb in range(nrb):
                tiles = []
                for t in range(nt):
                    delta = jnp.clip(c * nt + t - (i * nrb + rb), -C_BIAS_REACH, C_BIAS_REACH)
                    tiles.append(bias_ref[e, delta + C_BIAS_REACH])
                rows.append(jnp.concatenate(tiles, axis=1))
            b = jnp.concatenate(rows, axis=0)
            s_scr[slot, e] = s + jnp.concatenate([b, b], axis=0)

    def consume(c, slot):
        for e in range(2):
            v_ext = jnp.concatenate([v_ref[_chunk_rows(c, tk), e * LANES:(e + 1) * LANES], ones], axis=1)
            _flash_step(s_scr[slot, e], v_ext, m_scr.at[e], acc_scr.at[e], tk)

    _pipelined_chunks(T // tk, scores, consume)
    lv = lamv_ref[...]
    lam = (jnp.exp(jnp.sum(lv[0:1] * lv[1:2], axis=1, keepdims=True))
           - jnp.exp(jnp.sum(lv[2:3] * lv[3:4], axis=1, keepdims=True)) + lam_init)
    for e in range(2):
        o = acc_scr[e, :, :LANES] / acc_scr[e, :, LANES:]
        o = o[:tq] - lam * o[tq:]
        y = o * lax.rsqrt(jnp.mean(o * o, axis=-1, keepdims=True) + EPS) * subln_ref[...]
        o_ref[:, e * LANES:(e + 1) * LANES] = (y * (1.0 - lam_init)).astype(o_ref.dtype)


def _diff_attention(P, B, T, bias_c, lamv, subln, lam_init):
    tq = 256
    tk = min(1024, T // 2)
    nq = T // tq
    rows = 2 * tq
    return pl.pallas_call(
        functools.partial(_diff_kernel, tq=tq, tk=tk, T=T, lam_init=lam_init),
        out_shape=jax.ShapeDtypeStruct((B * T, 512), BF16),
        grid=(B, C_HEADS // 2, nq),
        in_specs=[
            pl.BlockSpec((tq, 256), lambda b, h, i: (b * nq + i, COL_QC // 256 + h)),
            pl.BlockSpec((T, 256), lambda b, h, i: (b, COL_KC // 256 + h)),
            pl.BlockSpec((T, 256), lambda b, h, i: (b, COL_VC // 256 + h)),
            pl.BlockSpec((2, 2 * C_BIAS_REACH + 1, LANES, LANES), lambda b, h, i: (h, 0, 0, 0)),
            pl.BlockSpec((4, HEAD_DIM), lambda b, h, i: (0, 0)),
            pl.BlockSpec((1, C_VDIM), lambda b, h, i: (0, 0)),
        ],
        out_specs=pl.BlockSpec((tq, 256), lambda b, h, i: (b * nq + i, h)),
        scratch_shapes=[pltpu.VMEM((2, rows, LANES), BF16), pltpu.VMEM((2, rows, LANES), F32),
                        pltpu.VMEM((2, rows, 2 * LANES), F32), pltpu.VMEM((2, 2, rows, tk), F32)],
        compiler_params=_cparams(("parallel", "parallel", "parallel")),
        name="diff_attn",
    )(P, P, P, bias_c, lamv, subln)


def _gated_out_kernel(*refs, n_a, gate_row):
    a_refs = refs[:n_a]
    w_ref, x_ref, mod_ref, o_ref = refs[n_a:]
    a = a_refs[0][...] if n_a == 1 else jnp.concatenate([r[...] for r in a_refs], axis=1)
    y = jnp.dot(a, w_ref[...], preferred_element_type=F32)
    o_ref[...] = x_ref[...] + mod_ref[0, gate_row:gate_row + 1, :] * y


def _gated_out(a_list, w, x2d, mod, gate_row, T):
    M = x2d.shape[0]
    tm, tn = 1024, 512
    nt = T // tm
    a_specs = [pl.BlockSpec((tm, a.shape[1]), lambda i, j: (i, 0)) for a in a_list]
    return pl.pallas_call(
        functools.partial(_gated_out_kernel, n_a=len(a_list), gate_row=gate_row),
        out_shape=jax.ShapeDtypeStruct((M, D_MODEL), F32),
        grid=(M // tm, D_MODEL // tn),
        in_specs=a_specs + [
            pl.BlockSpec((w.shape[0], tn), lambda i, j: (0, j)),
            pl.BlockSpec((tm, tn), lambda i, j: (i, j)),
            pl.BlockSpec((1, N_MOD, tn), lambda i, j: (i // nt, 0, j)),
        ],
        out_specs=pl.BlockSpec((tm, tn), lambda i, j: (i, j)),
        compiler_params=_cparams(("parallel", "arbitrary")),
        name=f"gated_out_{gate_row}",
    )(*a_list, w, x2d, mod)


def _ffn_up_kernel(x_ref, mod_ref, g_ref, wg_ref, wu_ref, o_ref, h_scr):
    @pl.when(pl.program_id(1) == 0)
    def _():
        mod = mod_ref[0]
        h_scr[...] = _modulated_norm(x_ref[...], g_ref[...], mod[3:4], mod[4:5]).astype(BF16)

    hb = h_scr[...]
    gate = jnp.dot(hb, wg_ref[...], preferred_element_type=F32)
    up = jnp.dot(hb, wu_ref[...], preferred_element_type=F32)
    o_ref[...] = (gate * jax.nn.sigmoid(gate) * up).astype(o_ref.dtype)


def _ffn_up(x2d, mod, g, w_gu, T):
    M = x2d.shape[0]
    tm, tn = 1024, 512
    nt = T // tm
    nj = FF_DIM // tn
    return pl.pallas_call(
        _ffn_up_kernel,
        out_shape=jax.ShapeDtypeStruct((M, FF_DIM), BF16),
        grid=(M // tm, nj),
        in_specs=[
            pl.BlockSpec((tm, D_MODEL), lambda i, j: (i, 0)),
            pl.BlockSpec((1, N_MOD, D_MODEL), lambda i, j: (i // nt, 0, 0)),
            pl.BlockSpec((1, D_MODEL), lambda i, j: (0, 0)),
            pl.BlockSpec((D_MODEL, tn), lambda i, j: (0, j)),
            pl.BlockSpec((D_MODEL, tn), lambda i, j: (0, j + nj)),
        ],
        out_specs=pl.BlockSpec((tm, tn), lambda i, j: (i, j)),
        scratch_shapes=[pltpu.VMEM((tm, D_MODEL), BF16)],
        compiler_params=_cparams(("parallel", "arbitrary")),
        name="ffn_up",
    )(x2d, mod, g, w_gu, w_gu)


def _relpos_bucket(rel):
    half = NUM_BUCKETS // 2
    max_exact = half // 2
    n = jnp.abs(rel)
    nf = jnp.maximum(n, 1).astype(F32)
    large = max_exact + (jnp.log(nf / max_exact) / math.log(RELPOS_MAX_DIST / max_exact)
                         * (half - max_exact)).astype(jnp.int32)
    large = jnp.minimum(large, half - 1)
    return jnp.where(rel > 0, half, 0) + jnp.where(n < max_exact, n, large)


def _relpos_bias(table, rel):
    bucket = _relpos_bucket(rel)[None]
    tab = table.astype(F32).reshape((NUM_BUCKETS, -1) + (1,) * rel.ndim)
    out = jnp.zeros((table.shape[1],) + rel.shape, F32)
    for b in range(NUM_BUCKETS):
        out = jnp.where(bucket == b, tab[b], out)
    return out


def _band_bias(table, sbq, hw, stride):
    col = jnp.arange(sbq + 2 * hw)[None, :]
    rel = col - hw - jnp.arange(sbq)[:, None]
    base = jnp.where((jnp.abs(rel) <= hw)[None], _relpos_bias(table, rel * stride) * LOG2E, NEG_INF)
    left, right = (col < hw)[None], (col >= sbq + hw)[None]
    return jnp.stack([base, jnp.where(left, NEG_INF, base), jnp.where(right, NEG_INF, base),
                      jnp.where(left | right, NEG_INF, base)])


def _dense_bias(table, blk):
    delta = jnp.arange(-C_BIAS_REACH, C_BIAS_REACH + 1)
    rel = delta[:, None, None] * blk + jnp.arange(blk)[None, None, :] - jnp.arange(blk)[None, :, None]
    return _relpos_bias(table, rel) * LOG2E


def _rope_tables(T):
    t = jnp.arange(T)
    pos = jnp.stack([t // GRID_W, t % GRID_W], axis=-1).astype(F32)
    inv_freq = ROPE_THETA ** (-jnp.arange(ROPE_PAIRS, dtype=F32) / ROPE_PAIRS)
    ang = pos[:, :, None] * inv_freq
    d = np.arange(LANES) % HEAD_DIM
    axis, half, pair = d // 32, (d % 32) // 16, d % 16
    cos_t = jnp.cos(ang)[:, axis, pair]
    sin_t = jnp.sin(ang)[:, axis, pair] * jnp.asarray(np.where(half == 0, -1.0, 1.0), F32)
    return cos_t, sin_t


def _dup_kv(w, off, pad=False):
    h0, h1 = w[..., off:off + 64], w[..., off + 64:off + 128]
    return [h0, jnp.zeros_like(h0), h1, jnp.zeros_like(h1)] if pad else [h0, h0, h1, h1]


def _prep_w_in(w_in):
    s = lambda a, b: w_in[..., a:b]
    cols = [s(0, 512), s(768, 1280), s(2304, 2816), s(3840, 4352),
            s(1280, 1792), s(2816, 3328), s(1792, 2304), s(3328, 3840)]
    cols += _dup_kv(w_in, 512) + _dup_kv(w_in, 4352) + _dup_kv(w_in, 640, pad=True) + _dup_kv(w_in, 4480, pad=True)
    return jnp.concatenate(cols, axis=-1).astype(BF16)


def _prep_gain(qk_gain):
    qs = HEAD_DIM ** -0.5 * LOG2E
    t = lambda g, n: jnp.tile(g, (1, n))
    ones = jnp.ones((DEPTH, 512), F32)
    rows = [t(qk_gain[:, 0, 0], 8) * qs, t(qk_gain[:, 1, 0], 8) * qs, t(qk_gain[:, 2, 0], 8) * qs,
            t(qk_gain[:, 3, 0], 8) * qs, t(qk_gain[:, 1, 1], 8), t(qk_gain[:, 2, 1], 8), ones, ones,
            t(qk_gain[:, 0, 1], 4), t(qk_gain[:, 3, 1], 4), ones]
    return jnp.concatenate(rows, axis=-1).reshape(DEPTH, 1, P_WIDTH)


def _trunk(x, mods, consts, weights):
    B, T, _ = x.shape
    M = B * T
    (norm_mix, norm_ffn, w_in_p, gain, w_out, sink_a, diff_lambda, diff_subln, w_gu, w_down) = weights
    bd, bias_a, bias_b, bias_c = consts
    cos_t, sin_t = _rope_tables(T)
    x2d = x.reshape(M, D_MODEL)
    for l in range(DEPTH):
        mod = mods[l]
        P, *P_strided = _proj_in(x2d, mod, norm_mix[l][None], w_in_p[l], gain[l], bd, cos_t, sin_t, T)
        (oa,) = _banded_attention(P, B, T, A_HALF_WIN, COL_QA, COL_KA, COL_VA, True, bias_a, sink_a[l] * LOG2E,
                                  "banded_a")
        window, _ = B_PATTERNS[0]
        o1, l1 = _banded_attention(P, B, T, window // 2, COL_QB, COL_KB, COL_VB, False, bias_b[0], None,
                                   "banded_b1")
        strided = []
        for (window, r), bias, Pr in zip(B_PATTERNS[1:], bias_b[1:], P_strided):
            L = T // r
            o, lse = _banded_attention(Pr.reshape(B * r * L, 1536), B * r, L, window // (2 * r), 0, 512, 1024,
                                       False, bias, None, f"banded_b{r}")
            strided.append((o.reshape(B, r, L, 512), lse.reshape(B, r, L, 512)))
        ob = _combine(o1, l1, strided, B, T)
        lam_init = 0.8 - 0.6 * math.exp(-0.3 * l)
        oc = _diff_attention(P, B, T, bias_c, diff_lambda[l], diff_subln[l][None], lam_init)
        od = _dense_gqa(P, B, T)
        x2d = _gated_out([oa, ob, oc, od], w_out[l], x2d, mod, 2, T)
        hf = _ffn_up(x2d, mod, norm_ffn[l][None], w_gu[l], T)
        x2d = _gated_out([hf], w_down[l], x2d, mod, 5, T)
    return x2d.reshape(B, T, D_MODEL)


def kernel(x_prompt, x_sample, c_prompt, c_sample, w_mod, b_mod, norm_mix, norm_ffn, w_in, w_out, qk_gain,
           sink_a, relpos_table, diff_lambda, diff_subln, w_gate_up, w_down):
    nb = x_prompt.shape[0]
    ns = x_sample.shape[0]
    c_all = jnp.concatenate([c_prompt, c_sample, jnp.zeros((8 - nb - ns, D_MODEL), F32)], axis=0)
    mods = _mod_all(c_all, w_mod, b_mod).reshape(DEPTH, 8, N_MOD, D_MODEL)

    idx = np.arange(256)
    bd = jnp.asarray((idx[:, None] // HEAD_DIM == idx[None, :] // HEAD_DIM) / HEAD_DIM, BF16)
    tab_a = relpos_table[:, :A_HEADS]
    tab_b = relpos_table[:, A_HEADS:A_HEADS + B_HEADS]
    tab_c = relpos_table[:, A_HEADS + B_HEADS:]
    weights = (norm_mix, norm_ffn, _prep_w_in(w_in), _prep_gain(qk_gain), w_out.astype(BF16), sink_a,
               diff_lambda, diff_subln, w_gate_up.astype(BF16), w_down.astype(BF16))

    outs = []
    for x, lo, hi in ((x_prompt, 0, nb), (x_sample, nb, nb + ns)):
        T = x.shape[1]
        bias_a = _band_bias(tab_a, min(128, T), A_HALF_WIN, 1)
        bias_b = [_band_bias(tab_b, min(128, T // r), w // (2 * r), r) for w, r in B_PATTERNS]
        bias_c = _dense_bias(tab_c, 128)
        outs.append(_trunk(x, mods[:, lo:hi], (bd, bias_a, bias_b, bias_c), weights))
    return tuple(outs)
```
